```python
import jax, jax.numpy as jnp
from jax import lax
import numpy as np

D_MODEL = 1024
BATCH = 16
SEQ = 4096
DEPTH = 2

N_BRANCH = 4
MIX = D_MODEL // 4
ML_HEADS = 4
ML_DH = MIX // ML_HEADS
ML_CHUNK = 64
GLA_HEADS = 4
GLA_DH = MIX // GLA_HEADS
GLA_RANK = 16
GLA_TAU = 16.0
GLA_CHUNK = 16
CONF_WIDTH = 31
SC_WIDTH = 3
FFN_WIDTH = 3
FFN_HIDDEN = ((8 * D_MODEL // 3 + 255) // 256) * 256
EPS = 1e-6

IN_SPLITS = (
    ("ml_q", MIX), ("ml_k", MIX), ("ml_v", MIX), ("ml_o", MIX),
    ("ml_i", ML_HEADS), ("ml_f", ML_HEADS),
    ("gla_q", MIX), ("gla_k", MIX), ("gla_v", MIX), ("gla_r", MIX), ("gla_a", GLA_RANK),
    ("conf", 2 * MIX),
    ("sc", 3 * MIX),
    ("gate", N_BRANCH * D_MODEL),
)
IN_WIDTH = sum(w for _, w in IN_SPLITS)

kernel_name = "hybrid_gated_mlstm_gla_conformer_shortconv_block"


def rms_norm(x, g):
    xf = x.astype(jnp.float32)
    y = xf * lax.rsqrt(jnp.mean(xf * xf, axis=-1, keepdims=True) + EPS)
    return (y * g.astype(jnp.float32)).astype(x.dtype)


def layer_norm(x, g, b):
    xf = x.astype(jnp.float32)
    mu = jnp.mean(xf, axis=-1, keepdims=True)
    xc = xf - mu
    y = xc * lax.rsqrt(jnp.mean(xc * xc, axis=-1, keepdims=True) + EPS)
    return (y * g.astype(jnp.float32) + b.astype(jnp.float32)).astype(x.dtype)


def head_rms_norm(t, g, n_heads):
    Bn, S, H, d = t.shape
    tf = t.astype(jnp.float32)
    y = tf * lax.rsqrt(jnp.mean(tf * tf, axis=-1, keepdims=True) + EPS)
    y = y * g.astype(jnp.float32).reshape(n_heads, d)
    return y.reshape(Bn, S, H * d)


def causal_dwconv(x, w):
    K, C = w.shape
    return lax.conv_general_dilated(
        x, w[:, None, :].astype(x.dtype), window_strides=(1,), padding=[(K - 1, 0)],
        dimension_numbers=("NWC", "WIO", "NWC"), feature_group_count=C)


def split_columns(proj):
    out = {}
    off = 0
    for name, width in IN_SPLITS:
        out[name] = proj[..., off:off + width]
        off += width
    return out


def split_heads(t, n_heads):
    return t.reshape(t.shape[:2] + (n_heads, t.shape[-1] // n_heads))


def to_chunks(t, L):
    Bn, S, H = t.shape[:3]
    t = t.reshape((Bn, S // L, L, H) + t.shape[3:])
    return jnp.moveaxis(t, (1, 3), (0, 2))


def from_chunks(t):
    NC, Bn, H, L = t.shape[:4]
    t = jnp.moveaxis(t, (0, 2), (1, 3))
    return t.reshape((Bn, NC * L, H) + t.shape[4:])


def mlstm_chunked(q, k, v, i_pre, f_pre):
    f32 = jnp.float32
    Bn, S, H, Dk = q.shape
    Dv = v.shape[-1]
    L = ML_CHUNK
    xs = (to_chunks(q.astype(f32) * (Dk ** -0.5), L), to_chunks(k.astype(f32), L),
          to_chunks(v.astype(f32), L), to_chunks(i_pre.astype(f32), L),
          to_chunks(jax.nn.log_sigmoid(f_pre.astype(f32)), L))
    causal = jnp.tril(jnp.ones((L, L), dtype=bool))

    def step(carry, chunk):
        C, n, m = carry
        qc, kc, vc, ic, fc = chunk
        b = jnp.cumsum(fc, axis=-1)
        logD = jnp.where(causal, b[..., :, None] - b[..., None, :] + ic[..., None, :], -jnp.inf)
        g = b + m[..., None]
        mt = jnp.maximum(g, jnp.max(logD, axis=-1))
        w_inter = jnp.exp(g - mt)
        qk = jnp.einsum('bhtd,bhsd->bhts', qc, kc) * jnp.exp(logD - mt[..., None])
        num = (w_inter[..., None] * jnp.einsum('bhtd,bhde->bhte', qc, C)
               + jnp.einsum('bhts,bhse->bhte', qk, vc))
        den = w_inter * jnp.einsum('bhtd,bhd->bht', qc, n) + jnp.sum(qk, axis=-1)
        h = num / jnp.maximum(jnp.abs(den), jnp.exp(-mt))[..., None]
        bL = b[..., -1]
        a = bL[..., None] - b + ic
        m_new = jnp.maximum(bL + m, jnp.max(a, axis=-1))
        w = jnp.exp(a - m_new[..., None])
        decay = jnp.exp(bL + m - m_new)
        C = decay[..., None, None] * C + jnp.einsum('bhs,bhsd,bhse->bhde', w, kc, vc)
        n = decay[..., None] * n + jnp.einsum('bhs,bhsd->bhd', w, kc)
        return (C, n, m_new), h

    init = (jnp.zeros((Bn, H, Dk, Dv), f32), jnp.zeros((Bn, H, Dk), f32), jnp.zeros((Bn, H), f32))
    _, h = lax.scan(step, init, xs)
    return from_chunks(h)


def gla_chunked(q, k, v, log_a):
    f32 = jnp.float32
    Bn, S, H, Dk = q.shape
    Dv = v.shape[-1]
    L = GLA_CHUNK
    xs = (to_chunks(q.astype(f32) * (Dk ** -0.5), L), to_chunks(k.astype(f32), L),
          to_chunks(v.astype(f32), L), to_chunks(log_a.astype(f32), L))
    causal = jnp.tril(jnp.ones((L, L), dtype=bool))

    def step(state, chunk):
        qc, kc, vc, lac = chunk
        b = jnp.cumsum(lac, axis=2)
        bL = b[:, :, -1:, :]
        q_dec = qc * jnp.exp(b)
        attn = jnp.where(causal, jnp.einsum('bhtd,bhsd->bhts', q_dec, kc * jnp.exp(-b)), 0.0)
        o = jnp.einsum('bhtd,bhde->bhte', q_dec, state) + jnp.einsum('bhts,bhse->bhte', attn, vc)
        k_dec = kc * jnp.exp(bL - b)
        state = jnp.exp(bL[:, :, 0, :])[..., None] * state + jnp.einsum('bhsd,bhse->bhde', k_dec, vc)
        return state, o

    _, o = lax.scan(step, jnp.zeros((Bn, H, Dk, Dv), f32), xs)
    return from_chunks(o)


def token_mixer(h, w_in, ml_i_bias, ml_f_bias, ml_norm_g, w_ml_out, gla_w_a2, gla_a_bias,
                gla_norm_g, w_gla_out, conf_dw_w, conf_dw_b, conf_ln_g, conf_ln_b, w_conf_out,
                conf_out_b, sc_dw_w, w_sc_out, merge_gate_b, w_o):
    Bn, S, _ = h.shape
    p = split_columns(h @ w_in)
    ml = mlstm_chunked(split_heads(p["ml_q"], ML_HEADS), split_heads(p["ml_k"], ML_HEADS),
                       split_heads(p["ml_v"], ML_HEADS), p["ml_i"] + ml_i_bias, p["ml_f"] + ml_f_bias)
    ml = head_rms_norm(ml, ml_norm_g, ML_HEADS).astype(h.dtype)
    y_ml = (ml * jax.nn.sigmoid(p["ml_o"])) @ w_ml_out
    log_a = jax.nn.log_sigmoid((p["gla_a"] @ gla_w_a2 + gla_a_bias).astype(jnp.float32)) / GLA_TAU
    gl = gla_chunked(split_heads(p["gla_q"], GLA_HEADS), split_heads(p["gla_k"], GLA_HEADS),
                     split_heads(p["gla_v"], GLA_HEADS), split_heads(log_a, GLA_HEADS))
    gl = head_rms_norm(gl, gla_norm_g, GLA_HEADS).astype(h.dtype)
    y_gla = (gl * jax.nn.silu(p["gla_r"])) @ w_gla_out
    a, g = jnp.split(p["conf"], 2, axis=-1)
    u = causal_dwconv(a * jax.nn.sigmoid(g), conf_dw_w) + conf_dw_b
    u = jax.nn.silu(layer_norm(u, conf_ln_g, conf_ln_b))
    y_conf = u @ w_conf_out + conf_out_b
    bg, cg, xv = jnp.split(p["sc"], 3, axis=-1)
    y_sc = (bg * causal_dwconv(cg * xv, sc_dw_w)) @ w_sc_out
    gates = jax.nn.sigmoid(p["gate"] + merge_gate_b).reshape(Bn, S, N_BRANCH, D_MODEL)
    merged = (gates[:, :, 0] * y_ml + gates[:, :, 1] * y_gla
              + gates[:, :, 2] * y_conf + gates[:, :, 3] * y_sc)
    return merged @ w_o


def conv_ffn(h, w_up, dw_w, w_down):
    u = causal_dwconv(h @ w_up, dw_w)
    a, v = jnp.split(u, 2, axis=-1)
    return (jax.nn.silu(a) * v) @ w_down


def setup_inputs(seed: int = 0) -> dict:
    key = jax.random.key(seed)
    keys = list(jax.random.split(key, 40))

    def nrm(shape, scale):
        return jax.random.normal(keys.pop(), shape, jnp.float32) * scale

    L = DEPTH
    D = D_MODEL
    return {
        "x": nrm((BATCH, SEQ, D), 1.0),
        "c": nrm((BATCH, D), 1.0),
        "ada_w": nrm((L, D, 6 * D), 0.5 * D ** -0.5),
        "ada_b": nrm((L, 6 * D), 0.02),
        "tm_pre_g": 1.0 + nrm((L, D), 0.05),
        "tm_post_g": 1.0 + nrm((L, D), 0.05),
        "cm_pre_g": 1.0 + nrm((L, D), 0.05),
        "cm_post_g": 1.0 + nrm((L, D), 0.05),
        "w_in": nrm((L, D, IN_WIDTH), D ** -0.5),
        "ml_i_bias": nrm((L, ML_HEADS), 0.1),
        "ml_f_bias": 3.0 + nrm((L, ML_HEADS), 0.5),
        "ml_norm_g": 1.0 + nrm((L, MIX), 0.05),
        "w_ml_out": nrm((L, MIX, D), MIX ** -0.5),
        "gla_w_a2": nrm((L, GLA_RANK, MIX), GLA_RANK ** -0.5),
        "gla_a_bias": nrm((L, MIX), 0.1),
        "gla_norm_g": 1.0 + nrm((L, MIX), 0.05),
        "w_gla_out": nrm((L, MIX, D), MIX ** -0.5),
        "conf_dw_w": nrm((L, CONF_WIDTH, MIX), CONF_WIDTH ** -0.5),
        "conf_dw_b": nrm((L, MIX), 0.02),
        "conf_ln_g": 1.0 + nrm((L, MIX), 0.05),
        "conf_ln_b": nrm((L, MIX), 0.02),
        "w_conf_out": nrm((L, MIX, D), MIX ** -0.5),
        "conf_out_b": nrm((L, D), 0.02),
        "sc_dw_w": nrm((L, SC_WIDTH, MIX), SC_WIDTH ** -0.5),
        "w_sc_out": nrm((L, MIX, D), MIX ** -0.5),
        "merge_gate_b": nrm((L, N_BRANCH * D), 0.02),
        "w_o": nrm((L, D, D), D ** -0.5),
        "ffn_w_up": nrm((L, D, 2 * FFN_HIDDEN), D ** -0.5),
        "ffn_dw_w": nrm((L, FFN_WIDTH, 2 * FFN_HIDDEN), FFN_WIDTH ** -0.5),
        "ffn_w_down": nrm((L, FFN_HIDDEN, D), FFN_HIDDEN ** -0.5),
    }


def reference(x, c, ada_w, ada_b, tm_pre_g, tm_post_g, cm_pre_g, cm_post_g, w_in, ml_i_bias,
              ml_f_bias, ml_norm_g, w_ml_out, gla_w_a2, gla_a_bias, gla_norm_g, w_gla_out,
              conf_dw_w, conf_dw_b, conf_ln_g, conf_ln_b, w_conf_out, conf_out_b, sc_dw_w,
              w_sc_out, merge_gate_b, w_o, ffn_w_up, ffn_dw_w, ffn_w_down):
    c_act = jax.nn.silu(c)
    for l in range(DEPTH):
        mod = (c_act @ ada_w[l] + ada_b[l])[:, None, :]
        sh_t, sc_t, g_t, sh_c, sc_c, g_c = jnp.split(mod, 6, axis=-1)
        h = rms_norm(x, tm_pre_g[l]) * (1.0 + sc_t) + sh_t
        y = token_mixer(h, w_in[l], ml_i_bias[l], ml_f_bias[l], ml_norm_g[l], w_ml_out[l],
                        gla_w_a2[l], gla_a_bias[l], gla_norm_g[l], w_gla_out[l], conf_dw_w[l],
                        conf_dw_b[l], conf_ln_g[l], conf_ln_b[l], w_conf_out[l], conf_out_b[l],
                        sc_dw_w[l], w_sc_out[l], merge_gate_b[l], w_o[l])
        x = x + g_t * rms_norm(y, tm_post_g[l])
        h = rms_norm(x, cm_pre_g[l]) * (1.0 + sc_c) + sh_c
        y = conv_ffn(h, ffn_w_up[l], ffn_dw_w[l], ffn_w_down[l])
        x = x + g_c * rms_norm(y, cm_post_g[l])
    return x
```

```python
import functools

import jax
import jax.numpy as jnp
from jax import lax
from jax.experimental import pallas as pl
from jax.experimental.pallas import tpu as pltpu

D_MODEL = 1024
MIX = 256
HEADS = 4
HEAD_DIM = 64
CHUNK = 64
SUB = 16
GLA_TAU = 16.0
CONF_WIDTH = 31
CONF_HALO = 32
CONV3_HALO = 8
FFN_HIDDEN = 2816
EPS = 1e-6
Q_SCALE = HEAD_DIM ** -0.5

TM_TILE = 256
FFN_TILE = 256
FFN_COLS = 256
V7X_VMEM_LIMIT_BYTES = 56 * 1024 * 1024

OFF_ML = 0
OFF_GLA = 1024
OFF_CONF = 2048
OFF_SC = 2560
OFF_IF = 3328
OFF_LOWRANK = 3840
OFF_GATE = 3968
PACKED_COLS = OFF_GATE + 4 * D_MODEL

BF16 = jnp.bfloat16
F32 = jnp.float32


def _sigmoid(x):
    return 0.5 * (jnp.tanh(0.5 * x) + 1.0)


def _log_sigmoid(x):
    return jnp.minimum(x, 0.0) - jnp.log(1.0 + jnp.exp(-jnp.abs(x)))


def _dot(a, b):
    return jnp.dot(a, b, preferred_element_type=F32)


def _dot_nt(a, b):
    return lax.dot_general(a, b, (((1,), (1,)), ((), ())), preferred_element_type=F32)


def _dot_tn(a, b):
    return lax.dot_general(a, b, (((0,), (0,)), ((), ())), preferred_element_type=F32)


def _scan_rows(x, op, fill):
    n = x.shape[0]
    row = lax.broadcasted_iota(jnp.int32, x.shape, 0)
    shift = 1
    while shift < n:
        x = op(x, jnp.where(row >= shift, pltpu.roll(x, shift, 0), fill))
        shift *= 2
    return x


def _rms_norm(x, g):
    return x * lax.rsqrt(jnp.mean(x * x, axis=-1, keepdims=True) + EPS) * g


def _mod_kernel(c_ref, w_ref, b_ref, o_ref):
    c = c_ref[...]
    c_act = (c * _sigmoid(c)).astype(BF16)
    o_ref[0] = _dot(c_act, w_ref[0].astype(BF16)) + b_ref[0]


def _modulation(c, ada_w, ada_b):
    depth, d, n = ada_w.shape
    bsz = c.shape[0]
    nb = 1536
    return pl.pallas_call(
        _mod_kernel,
        grid=(depth, n // nb),
        in_specs=[
            pl.BlockSpec((bsz, d), lambda l, j: (0, 0)),
            pl.BlockSpec((1, d, nb), lambda l, j: (l, 0, j)),
            pl.BlockSpec((1, 1, nb), lambda l, j: (l, 0, j)),
        ],
        out_specs=pl.BlockSpec((1, bsz, nb), lambda l, j: (l, 0, j)),
        out_shape=jax.ShapeDtypeStruct((depth, bsz, n), F32),
        compiler_params=pltpu.CompilerParams(
            dimension_semantics=("arbitrary", "arbitrary"), vmem_limit_bytes=V7X_VMEM_LIMIT_BYTES),
        name="adaln_modulation",
    )(c, ada_w, ada_b.reshape(depth, 1, n))


def _token_mixer_kernel(
        x_ref, shift_ref, scale_ref, gate_ref, pre_g_ref, post_g_ref, w_in_ref, if_bias_ref,
        w_a2_ref, a_bias_ref, ml_norm_g_ref, gla_norm_g_ref, w_branch_ref, conf_dw_w_ref,
        conf_dw_b_ref, conf_ln_g_ref, conf_ln_b_ref, conf_out_b_ref, sc_dw_w_ref, merge_b_ref,
        w_o_ref, out_ref,
        proj_ref, ml_h_ref, gla_h_ref, ml_state_ref, ml_m_ref, gla_state_ref, conf_buf_ref,
        sc_buf_ref):
    tile = x_ref.shape[1]
    step = pl.program_id(1)

    @pl.when(step == 0)
    def _():
        ml_state_ref[...] = jnp.zeros_like(ml_state_ref)
        ml_m_ref[...] = jnp.zeros_like(ml_m_ref)
        gla_state_ref[...] = jnp.zeros_like(gla_state_ref)
        conf_buf_ref[0:CONF_HALO, :] = jnp.zeros((CONF_HALO, MIX), F32)
        sc_buf_ref[0:CONV3_HALO, :] = jnp.zeros((CONV3_HALO, MIX), F32)

    x = x_ref[0]
    h = _rms_norm(x, pre_g_ref[...]) * (1.0 + scale_ref[0]) + shift_ref[0]
    h_bf = h.astype(BF16)

    proj_ref[...] = _dot(h_bf, w_in_ref[:, 0:OFF_GATE])

    row = lax.broadcasted_iota(jnp.int32, (CHUNK, MIX), 0)
    lane = lax.broadcasted_iota(jnp.int32, (CHUNK, MIX), 1)
    lane_head = lane >> 6
    lane_pos = lane & (HEAD_DIM - 1)
    causal = lane_pos <= row
    eye = lane_pos == row
    row_blk = row >> 4
    same_sub_causal = ((lane_pos >> 4) == row_blk) & causal
    sq_row = lax.broadcasted_iota(jnp.int32, (MIX, MIX), 0)
    sq_lane = lax.broadcasted_iota(jnp.int32, (MIX, MIX), 1)
    block_diag = (sq_row >> 6) == (sq_lane >> 6)
    block_diag2 = jnp.concatenate([block_diag, block_diag], axis=1)
    head_ones = jnp.where(block_diag, 1.0, 0.0).astype(BF16)

    def pick_row(a, i):
        return jnp.sum(jnp.where(row == i, a, 0.0), axis=0, keepdims=True)

    def expand_heads(a):
        reps = a.shape[1] // MIX
        parts = []
        for hd in range(HEADS):
            keep = lane_head == hd
            if reps > 1:
                keep = jnp.concatenate([keep] * reps, axis=1)
            parts.append(jnp.where(keep, a, 0.0).astype(BF16))
        return jnp.concatenate(parts, axis=0)

    i_bias = if_bias_ref[0:1, :]
    f_bias = if_bias_ref[1:2, :]
    ones_blk = jnp.ones((CHUNK, MIX), F32)

    for c in range(tile // CHUNK):
        rows = slice(c * CHUNK, (c + 1) * CHUNK)

        q = proj_ref[rows, OFF_ML:OFF_ML + MIX] * Q_SCALE
        k = proj_ref[rows, OFF_ML + MIX:OFF_ML + 2 * MIX]
        v = proj_ref[rows, OFF_ML + 2 * MIX:OFF_ML + 3 * MIX]
        i_pre = proj_ref[rows, OFF_IF:OFF_IF + MIX] + i_bias
        log_f = _log_sigmoid(proj_ref[rows, OFF_IF + MIX:OFF_IF + 2 * MIX] + f_bias)

        b = _scan_rows(log_f, jnp.add, 0.0)
        u = i_pre - b
        m_prev = ml_m_ref[...]
        m_run = jnp.maximum(m_prev, _scan_rows(u, jnp.maximum, -jnp.inf))
        u_lane = jnp.sum(jnp.where(eye, u, 0.0), axis=0, keepdims=True)
        decay_mat = jnp.exp(jnp.where(causal, u_lane - m_run, -jnp.inf))
        q_bf = q.astype(BF16)
        scores = _dot_nt(q_bf, expand_heads(k)) * decay_mat
        v_ones = jnp.concatenate([v, ones_blk], axis=1)
        intra = _dot(scores.astype(BF16), expand_heads(v_ones))
        state = ml_state_ref[...]
        inter = _dot(q_bf, state.astype(BF16))
        w_inter = jnp.exp(m_prev - m_run)
        num = w_inter * inter[:, 0:MIX] + intra[:, 0:MIX]
        den = w_inter * inter[:, MIX:2 * MIX] + intra[:, MIX:2 * MIX]
        ml_h_ref[rows, :] = num / jnp.maximum(jnp.abs(den), jnp.exp(-(m_run + b)))

        m_last = pick_row(m_run, CHUNK - 1)
        b_last = pick_row(b, CHUNK - 1)
        k_w = (k * jnp.exp(u - m_last)).astype(BF16)
        update = _dot_tn(k_w, v_ones.astype(BF16))
        decay = jnp.exp(m_prev - m_last)
        ml_state_ref[...] = (state * jnp.concatenate([decay, decay], axis=1)
                             + jnp.where(block_diag2, update, 0.0))
        ml_m_ref[...] = b_last + m_last

        gq = proj_ref[rows, OFF_GLA:OFF_GLA + MIX] * Q_SCALE
        gk = proj_ref[rows, OFF_GLA + MIX:OFF_GLA + 2 * MIX]
        gv = proj_ref[rows, OFF_GLA + 2 * MIX:OFF_GLA + 3 * MIX]
        low = proj_ref[rows, OFF_LOWRANK:OFF_LOWRANK + 128].astype(BF16)
        log_a = _log_sigmoid(_dot(low, w_a2_ref[...]) + a_bias_ref[...]) * (1.0 / GLA_TAU)

        gb = _scan_rows(log_a, jnp.add, 0.0)
        ends = [pick_row(gb, SUB * (j + 1) - 1) for j in range(CHUNK // SUB)]
        sub_start = jnp.where(row_blk == 0, 0.0,
                              jnp.where(row_blk == 1, ends[0], jnp.where(row_blk == 2, ends[1], ends[2])))
        sub_end = jnp.where(row_blk == 0, ends[0],
                            jnp.where(row_blk == 1, ends[1], jnp.where(row_blk == 2, ends[2], ends[3])))
        q_loc = (gq * jnp.exp(gb - sub_start)).astype(BF16)
        k_loc = gk * jnp.exp(sub_start - gb)
        attn = jnp.where(same_sub_causal, _dot_nt(q_loc, expand_heads(k_loc)), 0.0)
        k_end = gk * jnp.exp(sub_end - gb)
        for j in range(CHUNK // SUB - 1):
            q_j = jnp.where(row >= SUB * (j + 1), gq * jnp.exp(jnp.minimum(gb - ends[j], 0.0)), 0.0)
            k_j = jnp.where(row_blk == j, k_end, 0.0)
            attn = attn + _dot_nt(q_j.astype(BF16), expand_heads(k_j))
        g_state = gla_state_ref[...]
        g_out = (_dot_nt((gq * jnp.exp(gb)).astype(BF16), g_state.astype(BF16))
                 + _dot(attn.astype(BF16), expand_heads(gv)))
        gla_h_ref[rows, :] = g_out
        k_dec = (gk * jnp.exp(ends[3] - gb)).astype(BF16)
        gla_state_ref[...] = (g_state * jnp.exp(ends[3])
                              + jnp.where(block_diag, _dot_tn(gv.astype(BF16), k_dec), 0.0))

    def head_rms_norm(t, g):
        ms = _dot((t * t).astype(BF16), head_ones) * (1.0 / HEAD_DIM)
        return t * lax.rsqrt(ms + EPS) * g

    ml = head_rms_norm(ml_h_ref[...], ml_norm_g_ref[...])
    ml = ml * _sigmoid(proj_ref[:, OFF_ML + 3 * MIX:OFF_ML + 4 * MIX])
    merged = _sigmoid(_dot(h_bf, w_in_ref[:, OFF_GATE:OFF_GATE + D_MODEL]) + merge_b_ref[:, 0:D_MODEL]) \
        * _dot(ml.astype(BF16), w_branch_ref[0])

    gl = head_rms_norm(gla_h_ref[...], gla_norm_g_ref[...])
    r = proj_ref[:, OFF_GLA + 3 * MIX:OFF_GLA + 4 * MIX]
    gl = gl * (r * _sigmoid(r))
    merged = merged + _sigmoid(
        _dot(h_bf, w_in_ref[:, OFF_GATE + D_MODEL:OFF_GATE + 2 * D_MODEL])
        + merge_b_ref[:, D_MODEL:2 * D_MODEL]) * _dot(gl.astype(BF16), w_branch_ref[1])

    glu = proj_ref[:, OFF_CONF:OFF_CONF + MIX] * _sigmoid(proj_ref[:, OFF_CONF + MIX:OFF_CONF + 2 * MIX])
    conf_buf_ref[CONF_HALO:CONF_HALO + tile, :] = glu
    first = CONF_HALO - (CONF_WIDTH - 1)
    conv = conf_dw_b_ref[...] + conf_dw_w_ref[0:1, :] * conf_buf_ref[first:first + tile, :]
    for j in range(1, CONF_WIDTH):
        conv = conv + conf_dw_w_ref[j:j + 1, :] * conf_buf_ref[first + j:first + j + tile, :]
    conf_buf_ref[0:CONF_HALO, :] = conf_buf_ref[tile:tile + CONF_HALO, :]
    mu = jnp.mean(conv, axis=-1, keepdims=True)
    cen = conv - mu
    ln = cen * lax.rsqrt(jnp.mean(cen * cen, axis=-1, keepdims=True) + EPS) * conf_ln_g_ref[...] \
        + conf_ln_b_ref[...]
    cu = ln * _sigmoid(ln)
    merged = merged + _sigmoid(
        _dot(h_bf, w_in_ref[:, OFF_GATE + 2 * D_MODEL:OFF_GATE + 3 * D_MODEL])
        + merge_b_ref[:, 2 * D_MODEL:3 * D_MODEL]) * (
            _dot(cu.astype(BF16), w_branch_ref[2]) + conf_out_b_ref[...])

    cx = proj_ref[:, OFF_SC + MIX:OFF_SC + 2 * MIX] * proj_ref[:, OFF_SC + 2 * MIX:OFF_SC + 3 * MIX]
    sc_buf_ref[CONV3_HALO:CONV3_HALO + tile, :] = cx
    sconv = (sc_dw_w_ref[0:1, :] * sc_buf_ref[CONV3_HALO - 2:CONV3_HALO - 2 + tile, :]
             + sc_dw_w_ref[1:2, :] * sc_buf_ref[CONV3_HALO - 1:CONV3_HALO - 1 + tile, :]
             + sc_dw_w_ref[2:3, :] * cx)
    sc_buf_ref[0:CONV3_HALO, :] = sc_buf_ref[tile:tile + CONV3_HALO, :]
    sgated = proj_ref[:, OFF_SC:OFF_SC + MIX] * sconv
    merged = merged + _sigmoid(
        _dot(h_bf, w_in_ref[:, OFF_GATE + 3 * D_MODEL:OFF_GATE + 4 * D_MODEL])
        + merge_b_ref[:, 3 * D_MODEL:4 * D_MODEL]) * _dot(sgated.astype(BF16), w_branch_ref[3])

    y = _dot(merged.astype(BF16), w_o_ref[...])
    out_ref[0] = x + gate_ref[0] * _rms_norm(y, post_g_ref[...])


def _resident(shape):
    zeros = (0,) * len(shape)
    return pl.BlockSpec(shape, lambda b, t: zeros, pipeline_mode=pl.Buffered(1))


def _token_mixer(x, shift, scale, gate, pre_g, post_g, w_in, if_bias, w_a2, a_bias, ml_norm_g,
                 gla_norm_g, w_branch, conf_dw_w, conf_dw_b, conf_ln_g, conf_ln_b, conf_out_b,
                 sc_dw_w, merge_b, w_o):
    bsz, seq, d = x.shape
    tile = TM_TILE
    assert seq % tile == 0 and tile % CHUNK == 0 and d == D_MODEL
    x_spec = pl.BlockSpec((1, tile, d), lambda b, t: (b, t, 0))
    mod_spec = pl.BlockSpec((1, 1, d), lambda b, t: (b, 0, 0))
    params = (pre_g, post_g, w_in, if_bias, w_a2, a_bias, ml_norm_g, gla_norm_g, w_branch,
              conf_dw_w, conf_dw_b, conf_ln_g, conf_ln_b, conf_out_b, sc_dw_w, merge_b, w_o)
    return pl.pallas_call(
        _token_mixer_kernel,
        grid=(bsz, seq // tile),
        in_specs=[x_spec, mod_spec, mod_spec, mod_spec] + [_resident(p.shape) for p in params],
        out_specs=x_spec,
        out_shape=jax.ShapeDtypeStruct(x.shape, x.dtype),
        scratch_shapes=[
            pltpu.VMEM((tile, OFF_GATE), F32),
            pltpu.VMEM((tile, MIX), F32),
            pltpu.VMEM((tile, MIX), F32),
            pltpu.VMEM((MIX, 2 * MIX), F32),
            pltpu.VMEM((1, MIX), F32),
            pltpu.VMEM((MIX, MIX), F32),
            pltpu.VMEM((tile + CONF_HALO, MIX), F32),
            pltpu.VMEM((tile + CONV3_HALO, MIX), F32),
        ],
        compiler_params=pltpu.CompilerParams(
            dimension_semantics=("arbitrary", "arbitrary"), vmem_limit_bytes=V7X_VMEM_LIMIT_BYTES),
        name="token_mixer",
    )(x, shift, scale, gate, *params)


def _channel_mixer_kernel(x_ref, shift_ref, scale_ref, gate_ref, pre_g_ref, post_g_ref, w_up_ref,
                          dw_w_ref, w_down_ref, out_ref, u_buf_ref, act_ref):
    tile = x_ref.shape[1]

    @pl.when(pl.program_id(1) == 0)
    def _():
        u_buf_ref[0:CONV3_HALO, :] = jnp.zeros((CONV3_HALO, 2 * FFN_HIDDEN), F32)

    x = x_ref[0]
    h = _rms_norm(x, pre_g_ref[...]) * (1.0 + scale_ref[0]) + shift_ref[0]
    u_buf_ref[CONV3_HALO:CONV3_HALO + tile, :] = _dot(h.astype(BF16), w_up_ref[...])

    def conv3(cols):
        return (dw_w_ref[0:1, cols] * u_buf_ref[CONV3_HALO - 2:CONV3_HALO - 2 + tile, cols]
                + dw_w_ref[1:2, cols] * u_buf_ref[CONV3_HALO - 1:CONV3_HALO - 1 + tile, cols]
                + dw_w_ref[2:3, cols] * u_buf_ref[CONV3_HALO:CONV3_HALO + tile, cols])

    for n in range(FFN_HIDDEN // FFN_COLS):
        a = conv3(slice(n * FFN_COLS, (n + 1) * FFN_COLS))
        v = conv3(slice(FFN_HIDDEN + n * FFN_COLS, FFN_HIDDEN + (n + 1) * FFN_COLS))
        act_ref[:, n * FFN_COLS:(n + 1) * FFN_COLS] = (a * _sigmoid(a) * v).astype(BF16)
    u_buf_ref[0:CONV3_HALO, :] = u_buf_ref[tile:tile + CONV3_HALO, :]

    y = _dot(act_ref[...], w_down_ref[...])
    out_ref[0] = x + gate_ref[0] * _rms_norm(y, post_g_ref[...])


def _channel_mixer(x, shift, scale, gate, pre_g, post_g, w_up, dw_w, w_down):
    bsz, seq, d = x.shape
    tile = FFN_TILE
    assert seq % tile == 0 and d == D_MODEL
    x_spec = pl.BlockSpec((1, tile, d), lambda b, t: (b, t, 0))
    mod_spec = pl.BlockSpec((1, 1, d), lambda b, t: (b, 0, 0))
    params = (pre_g, post_g, w_up, dw_w, w_down)
    return pl.pallas_call(
        _channel_mixer_kernel,
        grid=(bsz, seq // tile),
        in_specs=[x_spec, mod_spec, mod_spec, mod_spec] + [_resident(p.shape) for p in params],
        out_specs=x_spec,
        out_shape=jax.ShapeDtypeStruct(x.shape, x.dtype),
        scratch_shapes=[
            pltpu.VMEM((tile + CONV3_HALO, 2 * FFN_HIDDEN), F32),
            pltpu.VMEM((tile, FFN_HIDDEN), BF16),
        ],
        compiler_params=pltpu.CompilerParams(
            dimension_semantics=("arbitrary", "arbitrary"), vmem_limit_bytes=V7X_VMEM_LIMIT_BYTES),
        name="channel_mixer",
    )(x, shift, scale, gate, *params)


def _pad_rows(a, n):
    return jnp.pad(a, ((0, n - a.shape[0]), (0, 0)))


def _pack_w_in(w):
    ml = w[:, 0:1024]
    ml_i = w[:, 1024:1028]
    ml_f = w[:, 1028:1032]
    gla = w[:, 1032:2056]
    gla_a = w[:, 2056:2072]
    conf = w[:, 2072:2584]
    sc = w[:, 2584:3352]
    gate = w[:, 3352:7448]
    packed = jnp.concatenate([
        ml, gla, conf, sc,
        jnp.repeat(ml_i, HEAD_DIM, axis=1), jnp.repeat(ml_f, HEAD_DIM, axis=1),
        jnp.pad(gla_a, ((0, 0), (0, 128 - gla_a.shape[1]))),
        gate], axis=1)
    assert packed.shape[1] == PACKED_COLS
    return packed.astype(BF16)


def kernel(x, c, ada_w, ada_b, tm_pre_g, tm_post_g, cm_pre_g, cm_post_g, w_in, ml_i_bias, ml_f_bias, ml_norm_g, w_ml_out, gla_w_a2, gla_a_bias, gla_norm_g, w_gla_out, conf_dw_w, conf_dw_b, conf_ln_g, conf_ln_b, w_conf_out, conf_out_b, sc_dw_w, w_sc_out, merge_gate_b, w_o, ffn_w_up, ffn_dw_w, ffn_w_down):
    bsz = x.shape[0]
    depth = ada_w.shape[0]
    d = D_MODEL
    mod = _modulation(c, ada_w, ada_b)

    def row(a):
        return a.reshape(1, -1)

    for l in range(depth):
        m = mod[l].reshape(bsz, 6, 1, d)
        sh_t, sc_t, g_t, sh_c, sc_c, g_c = (m[:, i] for i in range(6))
        if_bias = jnp.stack([jnp.repeat(ml_i_bias[l], HEAD_DIM), jnp.repeat(ml_f_bias[l], HEAD_DIM)])
        w_branch = jnp.stack([w_ml_out[l], w_gla_out[l], w_conf_out[l], w_sc_out[l]]).astype(BF16)
        x = _token_mixer(
            x, sh_t, sc_t, g_t, row(tm_pre_g[l]), row(tm_post_g[l]), _pack_w_in(w_in[l]),
            _pad_rows(if_bias, 8), _pad_rows(gla_w_a2[l], 128).astype(BF16), row(gla_a_bias[l]),
            row(ml_norm_g[l]), row(gla_norm_g[l]), w_branch, _pad_rows(conf_dw_w[l], CONF_HALO),
            row(conf_dw_b[l]), row(conf_ln_g[l]), row(conf_ln_b[l]), row(conf_out_b[l]),
            _pad_rows(sc_dw_w[l], 8), row(merge_gate_b[l]), w_o[l].astype(BF16))
        x = _channel_mixer(
            x, sh_c, sc_c, g_c, row(cm_pre_g[l]), row(cm_post_g[l]), ffn_w_up[l].astype(BF16),
            _pad_rows(ffn_dw_w[l], 8), ffn_w_down[l].astype(BF16))
    return x
```

```python
import functools

import jax
import jax.numpy as jnp
from jax import lax
from jax.experimental import pallas as pl
from jax.experimental.pallas import tpu as pltpu

D_MODEL = 1024
MIX = 256
HEADS = 4
HEAD_DIM = 64
CHUNK = 64
SUB = 16
GLA_TAU = 16.0
CONF_WIDTH = 31
CONF_HALO = 32
CONV3_HALO = 8
FFN_HIDDEN = 2816
EPS = 1e-6
Q_SCALE = HEAD_DIM ** -0.5

TM_TILE = 256
FFN_TILE = 256
FFN_COLS = 256
GATE_PIECE = 256
V7X_VMEM_LIMIT_BYTES = 56 * 1024 * 1024

OFF_ML = 0
OFF_GLA = 1024
OFF_CONF = 2048
OFF_SC = 2560
OFF_IF = 3328
OFF_LOWRANK = 3840
OFF_GATE = 3968
PACKED_COLS = OFF_GATE + 4 * D_MODEL

BF16 = jnp.bfloat16
F32 = jnp.float32


def _sigmoid(x):
    return 0.5 * (jnp.tanh(0.5 * x) + 1.0)


def _log_sigmoid(x):
    return jnp.minimum(x, 0.0) - jnp.log(1.0 + jnp.exp(-jnp.abs(x)))


def _dot(a, b):
    return jnp.dot(a, b, preferred_element_type=F32)


def _dot_proj(a, b):
    return jnp.dot(a, b, preferred_element_type=F32)


def _dot_gate(a, b):
    return jnp.dot(a, b, preferred_element_type=F32)


def _dot_branch(a, b):
    return jnp.dot(a, b, preferred_element_type=F32)


def _dot_wo(a, b):
    return jnp.dot(a, b, preferred_element_type=F32)


def _dot_nt(a, b):
    return lax.dot_general(a, b, (((1,), (1,)), ((), ())), preferred_element_type=F32)


def _dot_tn(a, b):
    return lax.dot_general(a, b, (((0,), (0,)), ((), ())), preferred_element_type=F32)


def _scan_rows(x, op, fill):
    n = x.shape[0]
    row = lax.broadcasted_iota(jnp.int32, x.shape, 0)
    shift = 1
    while shift < n:
        x = op(x, jnp.where(row >= shift, pltpu.roll(x, shift, 0), fill))
        shift *= 2
    return x


def _rms_norm(x, g):
    return x * lax.rsqrt(jnp.mean(x * x, axis=-1, keepdims=True) + EPS) * g


def _mod_kernel(c_ref, w_ref, b_ref, o_ref):
    c = c_ref[...]
    c_act = (c * _sigmoid(c)).astype(BF16)
    o_ref[0] = _dot(c_act, w_ref[0].astype(BF16)) + b_ref[0]


def _modulation(c, ada_w, ada_b):
    depth, d, n = ada_w.shape
    bsz = c.shape[0]
    nb = 1536
    return pl.pallas_call(
        _mod_kernel,
        grid=(depth, n // nb),
        in_specs=[
            pl.BlockSpec((bsz, d), lambda l, j: (0, 0)),
            pl.BlockSpec((1, d, nb), lambda l, j: (l, 0, j)),
            pl.BlockSpec((1, 1, nb), lambda l, j: (l, 0, j)),
        ],
        out_specs=pl.BlockSpec((1, bsz, nb), lambda l, j: (l, 0, j)),
        out_shape=jax.ShapeDtypeStruct((depth, bsz, n), F32),
        compiler_params=pltpu.CompilerParams(
            dimension_semantics=("arbitrary", "arbitrary"), vmem_limit_bytes=V7X_VMEM_LIMIT_BYTES),
        name="adaln_modulation",
    )(c, ada_w, ada_b.reshape(depth, 1, n))


def _token_mixer_kernel(
        x_ref, shift_ref, scale_ref, gate_ref, pre_g_ref, post_g_ref, w_in_ref, if_bias_ref,
        w_a2_ref, a_bias_ref, ml_norm_g_ref, gla_norm_g_ref, w_branch_ref, conf_dw_w_ref,
        conf_dw_b_ref, conf_ln_g_ref, conf_ln_b_ref, conf_out_b_ref, sc_dw_w_ref, merge_b_ref,
        w_o_ref, out_ref,
        proj_ref, gates_ref, branch_in_ref, ml_state_ref, ml_m_ref, gla_state_ref,
        conf_buf_ref, conv_part_ref, sc_buf_ref):
    tile = x_ref.shape[1]
    step = pl.program_id(1)

    @pl.when(step == 0)
    def _():
        ml_state_ref[...] = jnp.zeros_like(ml_state_ref)
        ml_m_ref[...] = jnp.zeros_like(ml_m_ref)
        gla_state_ref[...] = jnp.zeros_like(gla_state_ref)
        conf_buf_ref[0:CONF_HALO, :] = jnp.zeros((CONF_HALO, MIX), F32)
        conf_buf_ref[CONF_HALO + tile:CONF_HALO + tile + 8, :] = jnp.zeros((8, MIX), F32)
        sc_buf_ref[0:CONV3_HALO, :] = jnp.zeros((CONV3_HALO, MIX), F32)

    x = x_ref[0]
    h = _rms_norm(x, pre_g_ref[...]) * (1.0 + scale_ref[0]) + shift_ref[0]
    h_bf = h.astype(BF16)

    def project(col, width):
        proj_ref[:, col:col + width] = _dot_proj(h_bf, w_in_ref[:, col:col + width])

    def merge_gate(col, width):
        gates_ref[:, col:col + width] = jnp.tanh(
            _dot_gate(h_bf, w_in_ref[:, OFF_GATE + col:OFF_GATE + col + width]) + merge_b_ref[:, col:col + width]) + 1.0

    def conv_group(r):
        acc = None
        for a in range(CONF_HALO // 8 + 1):
            j = 8 * a + r - (CONF_HALO - CONF_WIDTH + 1)
            if 0 <= j < CONF_WIDTH:
                term = conf_dw_w_ref[j:j + 1, :] * conf_buf_ref[8 * a:8 * a + tile + 8, :]
                acc = term if acc is None else acc + term
        if r == 0:
            return acc[0:tile, :]
        conv_part_ref[r - 1] = acc
        return conv_part_ref[r - 1, r:r + tile, :]

    project(OFF_CONF, 2 * MIX)
    glu = proj_ref[:, OFF_CONF:OFF_CONF + MIX] * _sigmoid(proj_ref[:, OFF_CONF + MIX:OFF_CONF + 2 * MIX])
    conf_buf_ref[CONF_HALO:CONF_HALO + tile, :] = glu

    project(OFF_SC, 3 * MIX)
    conv = conf_dw_b_ref[...] + conv_group(0)
    conv = conv + conv_group(1)

    project(OFF_ML, 4 * MIX)
    conv = conv + conv_group(2)
    conv = conv + conv_group(3)
    conv = conv + conv_group(4)

    project(OFF_IF, 2 * MIX + 128)
    conv = conv + conv_group(5)
    conv = conv + conv_group(6)

    project(OFF_GLA, 4 * MIX)
    conv = conv + conv_group(7)
    conf_buf_ref[0:CONF_HALO, :] = conf_buf_ref[tile:tile + CONF_HALO, :]
    mu = jnp.mean(conv, axis=-1, keepdims=True)
    cen = conv - mu
    ln = cen * lax.rsqrt(jnp.mean(cen * cen, axis=-1, keepdims=True) + EPS) * conf_ln_g_ref[...] \
        + conf_ln_b_ref[...]
    branch_in_ref[2] = (ln * _sigmoid(ln)).astype(BF16)

    cx = proj_ref[:, OFF_SC + MIX:OFF_SC + 2 * MIX] * proj_ref[:, OFF_SC + 2 * MIX:OFF_SC + 3 * MIX]
    sc_buf_ref[CONV3_HALO:CONV3_HALO + tile, :] = cx
    sconv = (sc_dw_w_ref[0:1, :] * sc_buf_ref[CONV3_HALO - 2:CONV3_HALO - 2 + tile, :]
             + sc_dw_w_ref[1:2, :] * sc_buf_ref[CONV3_HALO - 1:CONV3_HALO - 1 + tile, :]
             + sc_dw_w_ref[2:3, :] * cx)
    sc_buf_ref[0:CONV3_HALO, :] = sc_buf_ref[tile:tile + CONV3_HALO, :]
    branch_in_ref[3] = (proj_ref[:, OFF_SC:OFF_SC + MIX] * sconv).astype(BF16)

    n_chunks = tile // CHUNK
    n_subs = CHUNK // SUB
    row = lax.broadcasted_iota(jnp.int32, (tile, MIX), 0)
    lane = lax.broadcasted_iota(jnp.int32, (tile, MIX), 1)
    pos = row & (CHUNK - 1)
    chunk_of_row = row >> 6
    sub_of_row = pos >> 4
    lane_head = lane >> 6
    lane_pos = lane & (HEAD_DIM - 1)
    causal = lane_pos <= pos
    eye = lane_pos == pos
    same_sub_causal = ((lane_pos >> 4) == sub_of_row) & causal
    sq_row = lax.broadcasted_iota(jnp.int32, (MIX, MIX), 0)
    sq_lane = lax.broadcasted_iota(jnp.int32, (MIX, MIX), 1)
    block_diag = (sq_row >> 6) == (sq_lane >> 6)
    block_diag2 = jnp.concatenate([block_diag, block_diag], axis=1)
    head_ones = jnp.where(block_diag, 1.0, 0.0).astype(BF16)

    def seg_scan(a, op, fill):
        shift = 1
        while shift < CHUNK:
            a = op(a, jnp.where(pos >= shift, pltpu.roll(a, shift, 0), fill))
            shift *= 2
        return a

    def group_rows(a, keep, group):
        w = a.shape[1]
        return jnp.sum(jnp.where(keep, a, 0.0).reshape(tile // group, group, w), axis=1, keepdims=True)

    def spread_rows(g, group):
        w = g.shape[2]
        return jnp.broadcast_to(g, (tile // group, group, w)).reshape(tile, w)

    def chunk_rows(vals):
        out = vals[-1]
        for c in range(n_chunks - 2, -1, -1):
            out = jnp.where(chunk_of_row[:, 0:vals[c].shape[1]] == c, vals[c], out)
        return out

    lane_head_bf = lane_head.astype(BF16)
    head_lanes = [lane_head_bf == hd for hd in range(HEADS)]

    def expand_heads(a):
        reps = a.shape[1] // MIX
        a_bf = a.astype(BF16)
        zero = jnp.zeros_like(a_bf)
        parts = []
        for hd in range(HEADS):
            keep = head_lanes[hd] if reps == 1 else jnp.concatenate([head_lanes[hd]] * reps, axis=1)
            parts.append(jnp.where(keep, a_bf, zero))
        return [jnp.concatenate([p[c * CHUNK:(c + 1) * CHUNK] for p in parts], axis=0)
                for c in range(n_chunks)]

    def per_chunk(fn):
        return jnp.concatenate([fn(c, slice(c * CHUNK, (c + 1) * CHUNK)) for c in range(n_chunks)], axis=0)

    last_of_chunk = pos == CHUNK - 1

    gate_cols = iter(range(0, 4 * D_MODEL, GATE_PIECE))

    def next_gate():
        col = next(gate_cols, None)
        if col is not None:
            merge_gate(col, GATE_PIECE)

    next_gate()

    q_bf = (proj_ref[:, OFF_ML:OFF_ML + MIX] * Q_SCALE).astype(BF16)
    k = proj_ref[:, OFF_ML + MIX:OFF_ML + 2 * MIX]
    v = proj_ref[:, OFF_ML + 2 * MIX:OFF_ML + 3 * MIX]
    i_pre = proj_ref[:, OFF_IF:OFF_IF + MIX] + if_bias_ref[0:1, :]
    log_f = _log_sigmoid(proj_ref[:, OFF_IF + MIX:OFF_IF + 2 * MIX] + if_bias_ref[1:2, :])
    next_gate()
    b = seg_scan(log_f, jnp.add, 0.0)
    u = i_pre - b
    next_gate()
    u_max = seg_scan(u, jnp.maximum, -jnp.inf)
    next_gate()

    gq = proj_ref[:, OFF_GLA:OFF_GLA + MIX] * Q_SCALE
    gk = proj_ref[:, OFF_GLA + MIX:OFF_GLA + 2 * MIX]
    gv = proj_ref[:, OFF_GLA + 2 * MIX:OFF_GLA + 3 * MIX]
    low = proj_ref[:, OFF_LOWRANK:OFF_LOWRANK + 128].astype(BF16)
    log_a = _log_sigmoid(_dot(low, w_a2_ref[...]) + a_bias_ref[...]) * (1.0 / GLA_TAU)
    next_gate()
    gb = seg_scan(log_a, jnp.add, 0.0)
    next_gate()
    sub_ends = group_rows(gb, (pos & (SUB - 1)) == SUB - 1, SUB)
    sub_end = spread_rows(sub_ends, SUB)
    sub_id = lax.broadcasted_iota(jnp.int32, (tile // SUB, 1, MIX), 0)
    prev_ends = jnp.concatenate([jnp.zeros((1, 1, MIX), F32), sub_ends[:-1]], axis=0)
    sub_start = spread_rows(jnp.where((sub_id & (n_subs - 1)) == 0, 0.0, prev_ends), SUB)
    ends_by_chunk = sub_ends.reshape(n_chunks, n_subs, MIX)
    next_gate()

    chunk_max = group_rows(u_max, last_of_chunk, CHUNK)
    chunk_b = group_rows(b, last_of_chunk, CHUNK)
    m_start = [ml_m_ref[...]]
    m_end = []
    for c in range(n_chunks):
        m_end.append(jnp.maximum(m_start[c], chunk_max[c]))
        m_start.append(chunk_b[c] + m_end[c])
    ml_m_ref[...] = m_start[n_chunks]
    m_in = chunk_rows(m_start[:n_chunks])
    m_run = jnp.maximum(m_in, u_max)
    k_exp = expand_heads(k)
    raw_scores = per_chunk(lambda c, rows: _dot_nt(q_bf[rows], k_exp[c]))
    next_gate()
    u_lane = spread_rows(group_rows(u, eye, CHUNK), CHUNK)
    decay_mat = jnp.exp(jnp.where(causal, u_lane - m_run, -jnp.inf))

    q_loc = (gq * jnp.exp(gb - sub_start)).astype(BF16)
    k_loc_exp = expand_heads(gk * jnp.exp(sub_start - gb))
    attn = jnp.where(same_sub_causal, per_chunk(lambda c, rows: _dot_nt(q_loc[rows], k_loc_exp[c])), 0.0)
    next_gate()

    scores_bf = (raw_scores * decay_mat).astype(BF16)
    v_ones = jnp.concatenate([v, jnp.ones((tile, MIX), F32)], axis=1)
    vo_exp = expand_heads(v_ones)
    intra = per_chunk(lambda c, rows: _dot(scores_bf[rows], vo_exp[c]))
    next_gate()

    k_end = gk * jnp.exp(sub_end - gb)
    for j in range(n_subs - 1):
        end_j = chunk_rows([ends_by_chunk[c, j:j + 1, :] for c in range(n_chunks)])
        q_j = jnp.where(sub_of_row > j, gq * jnp.exp(jnp.minimum(gb - end_j, 0.0)), 0.0).astype(BF16)
        k_j_exp = expand_heads(jnp.where(sub_of_row == j, k_end, 0.0))
        attn = attn + per_chunk(lambda c, rows: _dot_nt(q_j[rows], k_j_exp[c]))
        next_gate()

    k_w = (k * jnp.exp(u - chunk_rows(m_end))).astype(BF16)
    v_ones_bf = v_ones.astype(BF16)
    state = ml_state_ref[...]
    inter_parts = []
    for c in range(n_chunks):
        rows = slice(c * CHUNK, (c + 1) * CHUNK)
        inter_parts.append(_dot(q_bf[rows], state.astype(BF16)))
        decay = jnp.exp(m_start[c] - m_end[c])
        state = (state * jnp.concatenate([decay, decay], axis=1)
                 + jnp.where(block_diag2, _dot_tn(k_w[rows], v_ones_bf[rows]), 0.0))
    ml_state_ref[...] = state
    next_gate()

    attn_bf = attn.astype(BF16)
    gv_exp = expand_heads(gv)
    g_intra = per_chunk(lambda c, rows: _dot(attn_bf[rows], gv_exp[c]))
    next_gate()

    inter = jnp.concatenate(inter_parts, axis=0)
    w_inter = jnp.exp(m_in - m_run)
    num = w_inter * inter[:, 0:MIX] + intra[:, 0:MIX]
    den = w_inter * inter[:, MIX:2 * MIX] + intra[:, MIX:2 * MIX]
    ml_h = num / jnp.maximum(jnp.abs(den), jnp.exp(-(m_run + b)))

    chunk_ends = [ends_by_chunk[c, n_subs - 1:n_subs, :] for c in range(n_chunks)]
    q_dec = (gq * jnp.exp(gb)).astype(BF16)
    k_dec = (gk * jnp.exp(chunk_rows(chunk_ends) - gb)).astype(BF16)
    gv_bf = gv.astype(BF16)
    g_state = gla_state_ref[...]
    g_inter_parts = []
    for c in range(n_chunks):
        rows = slice(c * CHUNK, (c + 1) * CHUNK)
        g_inter_parts.append(_dot_nt(q_dec[rows], g_state.astype(BF16)))
        g_state = (g_state * jnp.exp(chunk_ends[c])
                   + jnp.where(block_diag, _dot_tn(gv_bf[rows], k_dec[rows]), 0.0))
    gla_state_ref[...] = g_state
    next_gate()
    gla_h = jnp.concatenate(g_inter_parts, axis=0) + g_intra
    for _ in gate_cols:
        raise AssertionError("merge-gate pieces left over")

    def head_rms_norm(t, g):
        ms = _dot((t * t).astype(BF16), head_ones) * (1.0 / HEAD_DIM)
        return t * lax.rsqrt(ms + EPS) * g

    ml = head_rms_norm(ml_h, ml_norm_g_ref[...])
    branch_in_ref[0] = (ml * _sigmoid(proj_ref[:, OFF_ML + 3 * MIX:OFF_ML + 4 * MIX])).astype(BF16)
    gl = head_rms_norm(gla_h, gla_norm_g_ref[...])
    r = proj_ref[:, OFF_GLA + 3 * MIX:OFF_GLA + 4 * MIX]
    branch_in_ref[1] = (gl * (r * _sigmoid(r))).astype(BF16)

    merged = gates_ref[:, 0:D_MODEL] * _dot_branch(branch_in_ref[0], w_branch_ref[0])
    merged = merged + gates_ref[:, D_MODEL:2 * D_MODEL] * _dot_branch(branch_in_ref[1], w_branch_ref[1])
    merged = merged + gates_ref[:, 2 * D_MODEL:3 * D_MODEL] * (
        _dot_branch(branch_in_ref[2], w_branch_ref[2]) + conf_out_b_ref[...])
    merged = merged + gates_ref[:, 3 * D_MODEL:4 * D_MODEL] * _dot_branch(branch_in_ref[3], w_branch_ref[3])

    y = _dot_wo(merged.astype(BF16), w_o_ref[...])
    out_ref[0] = x + gate_ref[0] * _rms_norm(y, post_g_ref[...])


def _resident(shape):
    zeros = (0,) * len(shape)
    return pl.BlockSpec(shape, lambda b, t: zeros, pipeline_mode=pl.Buffered(1))


def _token_mixer(x, shift, scale, gate, pre_g, post_g, w_in, if_bias, w_a2, a_bias, ml_norm_g,
                 gla_norm_g, w_branch, conf_dw_w, conf_dw_b, conf_ln_g, conf_ln_b, conf_out_b,
                 sc_dw_w, merge_b, w_o):
    bsz, seq, d = x.shape
    tile = TM_TILE
    assert seq % tile == 0 and tile % CHUNK == 0 and d == D_MODEL
    x_spec = pl.BlockSpec((1, tile, d), lambda b, t: (b, t, 0))
    mod_spec = pl.BlockSpec((1, 1, d), lambda b, t: (b, 0, 0))
    params = (pre_g, post_g, w_in, if_bias, w_a2, a_bias, ml_norm_g, gla_norm_g, w_branch,
              conf_dw_w, conf_dw_b, conf_ln_g, conf_ln_b, conf_out_b, sc_dw_w, merge_b, w_o)
    return pl.pallas_call(
        _token_mixer_kernel,
        grid=(bsz, seq // tile),
        in_specs=[x_spec, mod_spec, mod_spec, mod_spec] + [_resident(p.shape) for p in params],
        out_specs=x_spec,
        out_shape=jax.ShapeDtypeStruct(x.shape, x.dtype),
        scratch_shapes=[
            pltpu.VMEM((tile, OFF_GATE), F32),
            pltpu.VMEM((tile, 4 * D_MODEL), F32),
            pltpu.VMEM((4, tile, MIX), BF16),
            pltpu.VMEM((MIX, 2 * MIX), F32),
            pltpu.VMEM((1, MIX), F32),
            pltpu.VMEM((MIX, MIX), F32),
            pltpu.VMEM((tile + CONF_HALO + 8, MIX), F32),
            pltpu.VMEM((7, tile + 8, MIX), F32),
            pltpu.VMEM((tile + CONV3_HALO, MIX), F32),
        ],
        compiler_params=pltpu.CompilerParams(
            dimension_semantics=("arbitrary", "arbitrary"), vmem_limit_bytes=V7X_VMEM_LIMIT_BYTES),
        name="token_mixer",
    )(x, shift, scale, gate, *params)


def _channel_mixer_kernel(x_ref, shift_ref, scale_ref, gate_ref, pre_g_ref, post_g_ref, w_up_ref,
                          dw_w_ref, w_down_ref, out_ref, u_buf_ref, act_ref):
    tile = x_ref.shape[1]

    @pl.when(pl.program_id(1) == 0)
    def _():
        u_buf_ref[0:CONV3_HALO, :] = jnp.zeros((CONV3_HALO, 2 * FFN_HIDDEN), F32)

    x = x_ref[0]
    h = _rms_norm(x, pre_g_ref[...]) * (1.0 + scale_ref[0]) + shift_ref[0]
    u_buf_ref[CONV3_HALO:CONV3_HALO + tile, :] = _dot(h.astype(BF16), w_up_ref[...])

    def conv3(cols):
        return (dw_w_ref[0:1, cols] * u_buf_ref[CONV3_HALO - 2:CONV3_HALO - 2 + tile, cols]
                + dw_w_ref[1:2, cols] * u_buf_ref[CONV3_HALO - 1:CONV3_HALO - 1 + tile, cols]
                + dw_w_ref[2:3, cols] * u_buf_ref[CONV3_HALO:CONV3_HALO + tile, cols])

    for n in range(FFN_HIDDEN // FFN_COLS):
        a = conv3(slice(n * FFN_COLS, (n + 1) * FFN_COLS))
        v = conv3(slice(FFN_HIDDEN + n * FFN_COLS, FFN_HIDDEN + (n + 1) * FFN_COLS))
        act_ref[:, n * FFN_COLS:(n + 1) * FFN_COLS] = (a * _sigmoid(a) * v).astype(BF16)
    u_buf_ref[0:CONV3_HALO, :] = u_buf_ref[tile:tile + CONV3_HALO, :]

    y = _dot(act_ref[...], w_down_ref[...])
    out_ref[0] = x + gate_ref[0] * _rms_norm(y, post_g_ref[...])


def _channel_mixer(x, shift, scale, gate, pre_g, post_g, w_up, dw_w, w_down):
    bsz, seq, d = x.shape
    tile = FFN_TILE
    assert seq % tile == 0 and d == D_MODEL
    x_spec = pl.BlockSpec((1, tile, d), lambda b, t: (b, t, 0))
    mod_spec = pl.BlockSpec((1, 1, d), lambda b, t: (b, 0, 0))
    params = (pre_g, post_g, w_up, dw_w, w_down)
    return pl.pallas_call(
        _channel_mixer_kernel,
        grid=(bsz, seq // tile),
        in_specs=[x_spec, mod_spec, mod_spec, mod_spec] + [_resident(p.shape) for p in params],
        out_specs=x_spec,
        out_shape=jax.ShapeDtypeStruct(x.shape, x.dtype),
        scratch_shapes=[
            pltpu.VMEM((tile + CONV3_HALO, 2 * FFN_HIDDEN), F32),
            pltpu.VMEM((tile, FFN_HIDDEN), BF16),
        ],
        compiler_params=pltpu.CompilerParams(
            dimension_semantics=("arbitrary", "arbitrary"), vmem_limit_bytes=V7X_VMEM_LIMIT_BYTES),
        name="channel_mixer",
    )(x, shift, scale, gate, *params)


def _pad_rows(a, n):
    return jnp.pad(a, ((0, n - a.shape[0]), (0, 0)))


def _pack_w_in(w):
    ml = w[:, 0:1024]
    ml_i = w[:, 1024:1028]
    ml_f = w[:, 1028:1032]
    gla = w[:, 1032:2056]
    gla_a = w[:, 2056:2072]
    conf = w[:, 2072:2584]
    sc = w[:, 2584:3352]
    gate = w[:, 3352:7448]
    packed = jnp.concatenate([
        ml, gla, conf, sc,
        jnp.repeat(ml_i, HEAD_DIM, axis=1), jnp.repeat(ml_f, HEAD_DIM, axis=1),
        jnp.pad(gla_a, ((0, 0), (0, 128 - gla_a.shape[1]))),
        0.5 * gate], axis=1)
    assert packed.shape[1] == PACKED_COLS
    return packed.astype(BF16)


def kernel(x, c, ada_w, ada_b, tm_pre_g, tm_post_g, cm_pre_g, cm_post_g, w_in, ml_i_bias, ml_f_bias, ml_norm_g, w_ml_out, gla_w_a2, gla_a_bias, gla_norm_g, w_gla_out, conf_dw_w, conf_dw_b, conf_ln_g, conf_ln_b, w_conf_out, conf_out_b, sc_dw_w, w_sc_out, merge_gate_b, w_o, ffn_w_up, ffn_dw_w, ffn_w_down):
    bsz = x.shape[0]
    depth = ada_w.shape[0]
    d = D_MODEL
    mod = _modulation(c, ada_w, ada_b)

    def row(a):
        return a.reshape(1, -1)

    for l in range(depth):
        m = mod[l].reshape(bsz, 6, 1, d)
        sh_t, sc_t, g_t, sh_c, sc_c, g_c = (m[:, i] for i in range(6))
        if_bias = jnp.stack([jnp.repeat(ml_i_bias[l], HEAD_DIM), jnp.repeat(ml_f_bias[l], HEAD_DIM)])
        w_branch = jnp.stack([w_ml_out[l], w_gla_out[l], w_conf_out[l], w_sc_out[l]]).astype(BF16)
        x = _token_mixer(
            x, sh_t, sc_t, g_t, row(tm_pre_g[l]), row(tm_post_g[l]), _pack_w_in(w_in[l]),
            _pad_rows(if_bias, 8), _pad_rows(gla_w_a2[l], 128).astype(BF16), row(gla_a_bias[l]),
            row(ml_norm_g[l]), row(gla_norm_g[l]), w_branch, _pad_rows(conf_dw_w[l], CONF_HALO),
            row(conf_dw_b[l]), row(conf_ln_g[l]), row(conf_ln_b[l]), row(conf_out_b[l]),
            _pad_rows(sc_dw_w[l], 8), row(0.5 * merge_gate_b[l]), (0.5 * w_o[l]).astype(BF16))
        x = _channel_mixer(
            x, sh_c, sc_c, g_c, row(cm_pre_g[l]), row(cm_post_g[l]), ffn_w_up[l].astype(BF16),
            _pad_rows(ffn_dw_w[l], 8), ffn_w_down[l].astype(BF16))
    return x
```

```python
import jax
import jax.numpy as jnp
from jax import lax
from jax.experimental import pallas as pl
from jax.experimental.pallas import tpu as pltpu

D_MODEL = 1024
MIX = 256
HEADS = 4
HEAD_DIM = 64
CHUNK = 64
SUB = 16
GLA_TAU = 16.0
CONF_WIDTH = 31
CONF_HALO = 32
CONV3_HALO = 8
FFN_HIDDEN = 2816
EPS = 1e-6
Q_SCALE = HEAD_DIM ** -0.5

STREAMS = 1
SUB_TILE = 512
FFN_COLS = 256
GATE_PIECE = 256
DOWN_BLOCKS = 2
TM_LEAD = 1
FFN_LEAD = 3
V7X_VMEM_LIMIT_BYTES = 58 * 1024 * 1024

OFF_ML = 0
OFF_GLA = 1024
OFF_CONF = 2048
OFF_SC = 2560
OFF_SMALL = 3328
OFF_GATE = 3456
SMALL_I = 16
SMALL_F = 20
PACKED_COLS = OFF_GATE + 4 * D_MODEL

BF16 = jnp.bfloat16
F32 = jnp.float32


def _sigmoid(x):
    return 0.5 * (jnp.tanh(0.5 * x) + 1.0)


def _log_sigmoid(x):
    return jnp.minimum(x, 0.0) - jnp.log(1.0 + jnp.exp(-jnp.abs(x)))


def _dot(a, b):
    return jnp.dot(a, b, preferred_element_type=F32)


def _dot_nt(a, b):
    return lax.dot_general(a, b, (((1,), (1,)), ((), ())), preferred_element_type=F32)


def _dot_tn(a, b):
    return lax.dot_general(a, b, (((0,), (0,)), ((), ())), preferred_element_type=F32)


def _rms_norm(x, g):
    return x * lax.rsqrt(jnp.mean(x * x, axis=-1, keepdims=True) + EPS) * g


def _interleave(streams, lead):
    done = [False] * len(streams)
    turn = 0
    while not all(done):
        for i, stream in enumerate(streams):
            if not done[i] and turn >= i * lead:
                done[i] = next(stream, "end") == "end"
        turn += 1


def _mod_kernel(c_ref, w_ref, b_ref, o_ref):
    c = c_ref[...]
    c_act = (c * _sigmoid(c)).astype(BF16)
    o_ref[0] = _dot(c_act, w_ref[0].astype(BF16)) + b_ref[0]


def _modulation(c, ada_w, ada_b):
    depth, d, n = ada_w.shape
    bsz = c.shape[0]
    nb = 1536
    return pl.pallas_call(
        _mod_kernel,
        grid=(depth, n // nb),
        in_specs=[
            pl.BlockSpec((bsz, d), lambda l, j: (0, 0)),
            pl.BlockSpec((1, d, nb), lambda l, j: (l, 0, j)),
            pl.BlockSpec((1, 1, nb), lambda l, j: (l, 0, j)),
        ],
        out_specs=pl.BlockSpec((1, bsz, nb), lambda l, j: (l, 0, j)),
        out_shape=jax.ShapeDtypeStruct((depth, bsz, n), F32),
        compiler_params=pltpu.CompilerParams(
            dimension_semantics=("arbitrary", "arbitrary"), vmem_limit_bytes=V7X_VMEM_LIMIT_BYTES),
        name="adaln_modulation",
    )(c, ada_w, ada_b.reshape(depth, 1, n))


class _ColumnGroups:
    def __init__(self, groups):
        self._groups = groups

    def _locate(self, idx):
        rows, cols = idx
        for off, ref in self._groups:
            if off <= cols.start and cols.stop <= off + ref.shape[1]:
                return ref, (rows, slice(cols.start - off, cols.stop - off))
        raise IndexError(idx)

    def __getitem__(self, idx):
        ref, local = self._locate(idx)
        return ref[local]

    def __setitem__(self, idx, value):
        ref, local = self._locate(idx)
        ref[local] = value


def _token_mixer_kernel(
        x_ref, shift_ref, scale_ref, gate_ref, pre_g_ref, post_g_ref, w_in_ref, if_bias_ref, expand_ref,
        w_a2_ref, a_bias_ref, ml_norm_g_ref, gla_norm_g_ref, w_branch_ref, conf_dw_w_ref,
        conf_dw_b_ref, conf_ln_g_ref, conf_ln_b_ref, conf_out_b_ref, sc_dw_w_ref, merge_b_ref,
        w_o_ref, out_ref,
        proj_ml_ref, proj_gla_ref, proj_conf_ref, proj_sc_ref, proj_small_ref, gates_ref, branch_in_ref,
        ml_state_ref, ml_n_ref, ml_m_ref, gla_state_ref,
        conf_buf_ref, conv_part_ref, sc_buf_ref):
    tile = x_ref.shape[1]
    rows_n = tile // STREAMS

    @pl.when(pl.program_id(1) == 0)
    def _():
        ml_state_ref[...] = jnp.zeros_like(ml_state_ref)
        ml_n_ref[...] = jnp.zeros_like(ml_n_ref)
        ml_m_ref[...] = jnp.zeros_like(ml_m_ref)
        gla_state_ref[...] = jnp.zeros_like(gla_state_ref)
        conf_buf_ref[...] = jnp.zeros_like(conf_buf_ref)
        sc_buf_ref[0:CONV3_HALO, :] = jnp.zeros((CONV3_HALO, MIX), F32)

    proj_ref = _ColumnGroups(((OFF_ML, proj_ml_ref), (OFF_GLA, proj_gla_ref), (OFF_CONF, proj_conf_ref),
                              (OFF_SC, proj_sc_ref), (OFF_SMALL, proj_small_ref)))

    n_chunks = rows_n // CHUNK
    n_subs = CHUNK // SUB
    row = lax.broadcasted_iota(jnp.int32, (rows_n, MIX), 0)
    lane = lax.broadcasted_iota(jnp.int32, (rows_n, MIX), 1)
    pos = row & (CHUNK - 1)
    chunk_of_row = row >> 6
    sub_of_row = pos >> 4
    lane_head = lane >> 6
    lane_pos = lane & (HEAD_DIM - 1)
    causal = lane_pos <= pos
    eye = lane_pos == pos
    same_sub_causal = ((lane_pos >> 4) == sub_of_row) & causal
    last_of_chunk = pos == CHUNK - 1
    sq_row = lax.broadcasted_iota(jnp.int32, (MIX, MIX), 0)
    sq_lane = lax.broadcasted_iota(jnp.int32, (MIX, MIX), 1)
    block_diag = (sq_row >> 6) == (sq_lane >> 6)
    head_ones = jnp.where(block_diag, 1.0, 0.0).astype(BF16)
    lane128 = lax.broadcasted_iota(jnp.int32, (rows_n, 128), 1)
    lane_head_bf = lane_head.astype(BF16)
    head_lanes = [lane_head_bf == hd for hd in range(HEADS)]

    def seg_scan(a, op, fill):
        shift = 1
        while shift < CHUNK:
            a = op(a, jnp.where(pos >= shift, pltpu.roll(a, shift, 0), fill))
            shift *= 2
        return a

    def group_rows(a, keep, group):
        w = a.shape[1]
        return jnp.sum(jnp.where(keep, a, 0.0).reshape(rows_n // group, group, w), axis=1, keepdims=True)

    def spread_rows(g, group):
        w = g.shape[2]
        return jnp.broadcast_to(g, (rows_n // group, group, w)).reshape(rows_n, w)

    def chunk_rows(vals):
        out = vals[-1]
        for c in range(n_chunks - 2, -1, -1):
            out = jnp.where(chunk_of_row[:, 0:vals[c].shape[1]] == c, vals[c], out)
        return out

    def expand_heads(a):
        reps = a.shape[1] // MIX
        a_bf = a.astype(BF16)
        zero = jnp.zeros_like(a_bf)
        parts = []
        for hd in range(HEADS):
            keep = head_lanes[hd] if reps == 1 else jnp.concatenate([head_lanes[hd]] * reps, axis=1)
            parts.append(jnp.where(keep, a_bf, zero))
        return [jnp.concatenate([p[c * CHUNK:(c + 1) * CHUNK] for p in parts], axis=0)
                for c in range(n_chunks)]

    def per_chunk(fn):
        return jnp.concatenate([fn(c, slice(c * CHUNK, (c + 1) * CHUNK)) for c in range(n_chunks)], axis=0)

    def head_rms_norm(t, g):
        ms = _dot((t * t).astype(BF16), head_ones) * (1.0 / HEAD_DIM)
        return t * lax.rsqrt(ms + EPS) * g

    def sub_tile_program(s):
        r0 = s * rows_n
        rows = slice(r0, r0 + rows_n)
        is_last = s == STREAMS - 1

        x = x_ref[0, rows, :]
        h_bf = (_rms_norm(x, pre_g_ref[...]) * (1.0 + scale_ref[0]) + shift_ref[0]).astype(BF16)
        yield

        def project(col, width):
            proj_ref[rows, col:col + width] = _dot(h_bf, w_in_ref[:, col:col + width])

        def conv_group(r):
            acc = None
            for a in range(CONF_HALO // 8 + 1):
                j = 8 * a + r - (CONF_HALO - CONF_WIDTH + 1)
                if 0 <= j < CONF_WIDTH:
                    term = conf_dw_w_ref[j:j + 1, :] * conf_buf_ref[r0 + 8 * a:r0 + 8 * a + rows_n + 8, :]
                    acc = term if acc is None else acc + term
            if r == 0:
                return acc[0:rows_n, :]
            slot = s * 7 + r - 1
            conv_part_ref[slot] = acc
            return conv_part_ref[slot, r:r + rows_n, :]

        project(OFF_CONF, 2 * MIX)
        glu = proj_ref[rows, OFF_CONF:OFF_CONF + MIX] * _sigmoid(proj_ref[rows, OFF_CONF + MIX:OFF_CONF + 2 * MIX])
        conf_buf_ref[CONF_HALO + r0:CONF_HALO + r0 + rows_n, :] = glu

        project(OFF_SC, 3 * MIX)
        conv = conf_dw_b_ref[...] + conv_group(0)
        conv = conv + conv_group(1)

        project(OFF_ML, 4 * MIX)
        conv = conv + conv_group(2)
        conv = conv + conv_group(3)
        conv = conv + conv_group(4)

        project(OFF_SMALL, 128)
        conv = conv + conv_group(5)
        conv = conv + conv_group(6)

        project(OFF_GLA, 4 * MIX)
        conv = conv + conv_group(7)
        if is_last:
            conf_buf_ref[0:CONF_HALO, :] = conf_buf_ref[tile:tile + CONF_HALO, :]
        mu = jnp.mean(conv, axis=-1, keepdims=True)
        cen = conv - mu
        ln = cen * lax.rsqrt(jnp.mean(cen * cen, axis=-1, keepdims=True) + EPS) * conf_ln_g_ref[...] \
            + conf_ln_b_ref[...]
        branch_in_ref[2, rows, :] = (ln * _sigmoid(ln)).astype(BF16)

        cx = proj_ref[rows, OFF_SC + MIX:OFF_SC + 2 * MIX] * proj_ref[rows, OFF_SC + 2 * MIX:OFF_SC + 3 * MIX]
        sc0 = CONV3_HALO + r0
        sc_buf_ref[sc0:sc0 + rows_n, :] = cx
        sconv = (sc_dw_w_ref[0:1, :] * sc_buf_ref[sc0 - 2:sc0 - 2 + rows_n, :]
                 + sc_dw_w_ref[1:2, :] * sc_buf_ref[sc0 - 1:sc0 - 1 + rows_n, :]
                 + sc_dw_w_ref[2:3, :] * cx)
        if is_last:
            sc_buf_ref[0:CONV3_HALO, :] = sc_buf_ref[tile:tile + CONV3_HALO, :]
        branch_in_ref[3, rows, :] = (proj_ref[rows, OFF_SC:OFF_SC + MIX] * sconv).astype(BF16)
        yield

        gate_cols = iter(range(0, 4 * D_MODEL, GATE_PIECE))

        def next_gate():
            col = next(gate_cols, None)
            if col is not None:
                z = _dot(h_bf, w_in_ref[:, OFF_GATE + col:OFF_GATE + col + GATE_PIECE])
                gates_ref[rows, col:col + GATE_PIECE] = jnp.tanh(z + merge_b_ref[:, col:col + GATE_PIECE]) + 1.0

        next_gate()
        q = proj_ref[rows, OFF_ML:OFF_ML + MIX] * Q_SCALE
        q_bf = q.astype(BF16)
        k = proj_ref[rows, OFF_ML + MIX:OFF_ML + 2 * MIX]
        v = proj_ref[rows, OFF_ML + 2 * MIX:OFF_ML + 3 * MIX]
        small = proj_ref[rows, OFF_SMALL:OFF_SMALL + 128]
        small_pre = small + if_bias_ref[0:1, :]
        compact = jnp.where((lane128 >= SMALL_F) & (lane128 < SMALL_F + HEADS), _log_sigmoid(small_pre), small_pre)
        compact_hi = compact.astype(BF16)
        compact_lo = (compact - compact_hi.astype(F32)).astype(BF16)
        spread = _dot(jnp.concatenate([compact_hi, compact_lo], axis=1), expand_ref[...])
        i_pre = spread[:, 0:MIX]
        log_f = spread[:, MIX:2 * MIX]
        next_gate()
        b = seg_scan(log_f, jnp.add, 0.0)
        u = i_pre - b
        next_gate()
        u_max = seg_scan(u, jnp.maximum, -jnp.inf)
        next_gate()

        gq = proj_ref[rows, OFF_GLA:OFF_GLA + MIX] * Q_SCALE
        gk = proj_ref[rows, OFF_GLA + MIX:OFF_GLA + 2 * MIX]
        gv = proj_ref[rows, OFF_GLA + 2 * MIX:OFF_GLA + 3 * MIX]
        low = small.astype(BF16)
        log_a = _log_sigmoid(_dot(low, w_a2_ref[...]) + a_bias_ref[...]) * (1.0 / GLA_TAU)
        next_gate()
        gb = seg_scan(log_a, jnp.add, 0.0)
        next_gate()
        sub_ends = group_rows(gb, (pos & (SUB - 1)) == SUB - 1, SUB)
        sub_end = spread_rows(sub_ends, SUB)
        sub_id = lax.broadcasted_iota(jnp.int32, (rows_n // SUB, 1, MIX), 0)
        prev_ends = jnp.concatenate([jnp.zeros((1, 1, MIX), F32), sub_ends[:-1]], axis=0)
        sub_start = spread_rows(jnp.where((sub_id & (n_subs - 1)) == 0, 0.0, prev_ends), SUB)
        ends_by_chunk = sub_ends.reshape(n_chunks, n_subs, MIX)
        next_gate()
        yield

        chunk_max = group_rows(u_max, last_of_chunk, CHUNK)
        chunk_b = group_rows(b, last_of_chunk, CHUNK)
        m_start = [ml_m_ref[...]]
        m_end = []
        for c in range(n_chunks):
            m_end.append(jnp.maximum(m_start[c], chunk_max[c]))
            m_start.append(chunk_b[c] + m_end[c])
        ml_m_ref[...] = m_start[n_chunks]
        m_in = chunk_rows(m_start[:n_chunks])
        m_run = jnp.maximum(m_in, u_max)
        k_exp = expand_heads(k)
        raw_scores = per_chunk(lambda c, cr: _dot_nt(q_bf[cr], k_exp[c]))
        next_gate()
        u_lane = spread_rows(group_rows(u, eye, CHUNK), CHUNK)
        decay_mat = jnp.exp(jnp.where(causal, u_lane - m_run, -jnp.inf))

        q_loc = (gq * jnp.exp(gb - sub_start)).astype(BF16)
        k_loc_exp = expand_heads(gk * jnp.exp(sub_start - gb))
        attn = jnp.where(same_sub_causal, per_chunk(lambda c, cr: _dot_nt(q_loc[cr], k_loc_exp[c])), 0.0)
        next_gate()

        scores_bf = (raw_scores * decay_mat).astype(BF16)
        v_exp = expand_heads(v)
        intra = per_chunk(lambda c, cr: _dot(scores_bf[cr], v_exp[c]))
        den_intra = _dot(scores_bf, head_ones)
        next_gate()

        k_end = gk * jnp.exp(sub_end - gb)
        for j in range(n_subs - 1):
            end_j = chunk_rows([ends_by_chunk[c, j:j + 1, :] for c in range(n_chunks)])
            q_j = jnp.where(sub_of_row > j, gq * jnp.exp(jnp.minimum(gb - end_j, 0.0)), 0.0).astype(BF16)
            k_j_exp = expand_heads(jnp.where(sub_of_row == j, k_end, 0.0))
            attn = attn + per_chunk(lambda c, cr: _dot_nt(q_j[cr], k_j_exp[c]))
            next_gate()
        yield

        k_w = k * jnp.exp(u - chunk_rows(m_end))
        k_w_bf = k_w.astype(BF16)
        k_w_sums = jnp.sum(k_w.reshape(n_chunks, CHUNK, MIX), axis=1, keepdims=True)
        v_bf = v.astype(BF16)
        state = ml_state_ref[...]
        norm_vec = ml_n_ref[...]
        inter_parts = []
        norm_in = []
        for c in range(n_chunks):
            cr = slice(c * CHUNK, (c + 1) * CHUNK)
            inter_parts.append(_dot(q_bf[cr], state.astype(BF16)))
            norm_in.append(norm_vec)
            decay = jnp.exp(m_start[c] - m_end[c])
            state = state * decay + jnp.where(block_diag, _dot_tn(k_w_bf[cr], v_bf[cr]), 0.0)
            norm_vec = norm_vec * decay + k_w_sums[c]
        ml_state_ref[...] = state
        ml_n_ref[...] = norm_vec
        den_inter = _dot((q * chunk_rows(norm_in)).astype(BF16), head_ones)
        next_gate()

        attn_bf = attn.astype(BF16)
        gv_exp = expand_heads(gv)
        g_intra = per_chunk(lambda c, cr: _dot(attn_bf[cr], gv_exp[c]))
        next_gate()

        inter = jnp.concatenate(inter_parts, axis=0)
        w_inter = jnp.exp(m_in - m_run)
        num = w_inter * inter + intra
        den = w_inter * den_inter + den_intra
        ml_h = num / jnp.maximum(jnp.abs(den), jnp.exp(-(m_run + b)))

        chunk_ends = [ends_by_chunk[c, n_subs - 1:n_subs, :] for c in range(n_chunks)]
        q_dec = (gq * jnp.exp(gb)).astype(BF16)
        k_dec = (gk * jnp.exp(chunk_rows(chunk_ends) - gb)).astype(BF16)
        gv_bf = gv.astype(BF16)
        g_state = gla_state_ref[...]
        g_inter_parts = []
        for c in range(n_chunks):
            cr = slice(c * CHUNK, (c + 1) * CHUNK)
            g_inter_parts.append(_dot_nt(q_dec[cr], g_state.astype(BF16)))
            g_state = (g_state * jnp.exp(chunk_ends[c])
                       + jnp.where(block_diag, _dot_tn(gv_bf[cr], k_dec[cr]), 0.0))
        gla_state_ref[...] = g_state
        next_gate()
        gla_h = jnp.concatenate(g_inter_parts, axis=0) + g_intra
        assert next(gate_cols, None) is None
        yield

        ml = head_rms_norm(ml_h, ml_norm_g_ref[...])
        ml_in = (ml * _sigmoid(proj_ref[rows, OFF_ML + 3 * MIX:OFF_ML + 4 * MIX])).astype(BF16)
        gl = head_rms_norm(gla_h, gla_norm_g_ref[...])
        r = proj_ref[rows, OFF_GLA + 3 * MIX:OFF_GLA + 4 * MIX]
        gl_in = (gl * (r * _sigmoid(r))).astype(BF16)

        merged = gates_ref[rows, 0:D_MODEL] * _dot(ml_in, w_branch_ref[0])
        merged = merged + gates_ref[rows, D_MODEL:2 * D_MODEL] * _dot(gl_in, w_branch_ref[1])
        merged = merged + gates_ref[rows, 2 * D_MODEL:3 * D_MODEL] * (
            _dot(branch_in_ref[2, rows, :], w_branch_ref[2]) + conf_out_b_ref[...])
        merged = merged + gates_ref[rows, 3 * D_MODEL:4 * D_MODEL] * _dot(branch_in_ref[3, rows, :], w_branch_ref[3])
        merged_bf = merged.astype(BF16)
        yield

        y = _dot(merged_bf, w_o_ref[...])
        yield

        out_ref[0, rows, :] = x + gate_ref[0] * _rms_norm(y, post_g_ref[...])

    _interleave([sub_tile_program(s) for s in range(STREAMS)], TM_LEAD)


def _resident(shape):
    zeros = (0,) * len(shape)
    return pl.BlockSpec(shape, lambda b, t: zeros, pipeline_mode=pl.Buffered(1))


def _token_mixer(x, shift, scale, gate, pre_g, post_g, w_in, if_bias, expand, w_a2, a_bias, ml_norm_g,
                 gla_norm_g, w_branch, conf_dw_w, conf_dw_b, conf_ln_g, conf_ln_b, conf_out_b,
                 sc_dw_w, merge_b, w_o):
    bsz, seq, d = x.shape
    tile = STREAMS * SUB_TILE
    assert seq % tile == 0 and SUB_TILE % CHUNK == 0 and d == D_MODEL
    x_spec = pl.BlockSpec((1, tile, d), lambda b, t: (b, t, 0))
    mod_spec = pl.BlockSpec((1, 1, d), lambda b, t: (b, 0, 0))
    params = (pre_g, post_g, w_in, if_bias, expand, w_a2, a_bias, ml_norm_g, gla_norm_g, w_branch,
              conf_dw_w, conf_dw_b, conf_ln_g, conf_ln_b, conf_out_b, sc_dw_w, merge_b, w_o)
    return pl.pallas_call(
        _token_mixer_kernel,
        grid=(bsz, seq // tile),
        in_specs=[x_spec, mod_spec, mod_spec, mod_spec] + [_resident(p.shape) for p in params],
        out_specs=x_spec,
        out_shape=jax.ShapeDtypeStruct(x.shape, x.dtype),
        scratch_shapes=[
            pltpu.VMEM((tile, 4 * MIX), F32),
            pltpu.VMEM((tile, 4 * MIX), F32),
            pltpu.VMEM((tile, 2 * MIX), F32),
            pltpu.VMEM((tile, 3 * MIX), F32),
            pltpu.VMEM((tile, 128), F32),
            pltpu.VMEM((tile, 4 * D_MODEL), F32),
            pltpu.VMEM((4, tile, MIX), BF16),
            pltpu.VMEM((MIX, MIX), F32),
            pltpu.VMEM((1, MIX), F32),
            pltpu.VMEM((1, MIX), F32),
            pltpu.VMEM((MIX, MIX), F32),
            pltpu.VMEM((CONF_HALO + tile + 8, MIX), F32),
            pltpu.VMEM((STREAMS * 7, SUB_TILE + 8, MIX), F32),
            pltpu.VMEM((CONV3_HALO + tile, MIX), F32),
        ],
        compiler_params=pltpu.CompilerParams(
            dimension_semantics=("arbitrary", "arbitrary"), vmem_limit_bytes=V7X_VMEM_LIMIT_BYTES),
        name="token_mixer",
    )(x, shift, scale, gate, *params)


def _channel_mixer_kernel(x_ref, shift_ref, scale_ref, gate_ref, pre_g_ref, post_g_ref, w_up_ref,
                          dw_w_ref, w_down_ref, out_ref, u_buf_ref, act_ref):
    tile = x_ref.shape[1]
    rows_n = tile // STREAMS
    n_blocks = FFN_HIDDEN // FFN_COLS

    @pl.when(pl.program_id(1) == 0)
    def _():
        u_buf_ref[0:CONV3_HALO, :] = jnp.zeros((CONV3_HALO, 2 * FFN_HIDDEN), F32)

    def sub_tile_program(s):
        r0 = s * rows_n
        rows = slice(r0, r0 + rows_n)
        u0 = CONV3_HALO + r0

        x = x_ref[0, rows, :]
        h_bf = (_rms_norm(x, pre_g_ref[...]) * (1.0 + scale_ref[0]) + shift_ref[0]).astype(BF16)
        yield

        def conv3(cols):
            return (dw_w_ref[0:1, cols] * u_buf_ref[u0 - 2:u0 - 2 + rows_n, cols]
                    + dw_w_ref[1:2, cols] * u_buf_ref[u0 - 1:u0 - 1 + rows_n, cols]
                    + dw_w_ref[2:3, cols] * u_buf_ref[u0:u0 + rows_n, cols])

        for n in range(n_blocks):
            cols_a = slice(n * FFN_COLS, (n + 1) * FFN_COLS)
            cols_v = slice(FFN_HIDDEN + n * FFN_COLS, FFN_HIDDEN + (n + 1) * FFN_COLS)
            u_buf_ref[u0:u0 + rows_n, cols_a] = _dot(h_bf, w_up_ref[:, cols_a])
            u_buf_ref[u0:u0 + rows_n, cols_v] = _dot(h_bf, w_up_ref[:, cols_v])
            yield
            a = conv3(cols_a)
            v = conv3(cols_v)
            act_ref[rows, cols_a] = (a * _sigmoid(a) * v).astype(BF16)
            if s == STREAMS - 1:
                for cols in (cols_a, cols_v):
                    u_buf_ref[0:CONV3_HALO, cols] = u_buf_ref[tile:tile + CONV3_HALO, cols]
            yield

        y = None
        for n in range(0, n_blocks, DOWN_BLOCKS):
            k_rows = slice(n * FFN_COLS, min(n + DOWN_BLOCKS, n_blocks) * FFN_COLS)
            part = _dot(act_ref[rows, k_rows], w_down_ref[k_rows, :])
            y = part if y is None else y + part
            yield

        out_ref[0, rows, :] = x + gate_ref[0] * _rms_norm(y, post_g_ref[...])

    _interleave([sub_tile_program(s) for s in range(STREAMS)], FFN_LEAD)


def _channel_mixer(x, shift, scale, gate, pre_g, post_g, w_up, dw_w, w_down):
    bsz, seq, d = x.shape
    tile = STREAMS * SUB_TILE
    assert seq % tile == 0 and d == D_MODEL
    x_spec = pl.BlockSpec((1, tile, d), lambda b, t: (b, t, 0))
    mod_spec = pl.BlockSpec((1, 1, d), lambda b, t: (b, 0, 0))
    params = (pre_g, post_g, w_up, dw_w, w_down)
    return pl.pallas_call(
        _channel_mixer_kernel,
        grid=(bsz, seq // tile),
        in_specs=[x_spec, mod_spec, mod_spec, mod_spec] + [_resident(p.shape) for p in params],
        out_specs=x_spec,
        out_shape=jax.ShapeDtypeStruct(x.shape, x.dtype),
        scratch_shapes=[
            pltpu.VMEM((CONV3_HALO + tile, 2 * FFN_HIDDEN), F32),
            pltpu.VMEM((tile, FFN_HIDDEN), BF16),
        ],
        compiler_params=pltpu.CompilerParams(
            dimension_semantics=("arbitrary", "arbitrary"), vmem_limit_bytes=V7X_VMEM_LIMIT_BYTES),
        name="channel_mixer",
    )(x, shift, scale, gate, *params)


def _pad_rows(a, n):
    return jnp.pad(a, ((0, n - a.shape[0]), (0, 0)))


def _pack_w_in(w):
    ml = w[:, 0:1024]
    ml_i = w[:, 1024:1028]
    ml_f = w[:, 1028:1032]
    gla = w[:, 1032:2056]
    gla_a = w[:, 2056:2072]
    conf = w[:, 2072:2584]
    sc = w[:, 2584:3352]
    gate = w[:, 3352:7448]
    small = jnp.concatenate([gla_a, ml_i, ml_f], axis=1)
    assert gla_a.shape[1] == SMALL_I and SMALL_F == SMALL_I + HEADS
    packed = jnp.concatenate([
        ml, gla, conf, sc, jnp.pad(small, ((0, 0), (0, 128 - small.shape[1]))), 0.5 * gate], axis=1)
    assert packed.shape[1] == PACKED_COLS
    return packed.astype(BF16)


def _gate_spread_matrix():
    lane = jnp.arange(128)[:, None]
    col = jnp.arange(2 * MIX)[None, :]
    is_forget = col >= MIX
    head = (col % MIX) // HEAD_DIM
    one = lane == jnp.where(is_forget, SMALL_F, SMALL_I) + head
    half = jnp.where(one, 1.0, 0.0).astype(BF16)
    return jnp.concatenate([half, half], axis=0)


def kernel(x, c, ada_w, ada_b, tm_pre_g, tm_post_g, cm_pre_g, cm_post_g, w_in, ml_i_bias, ml_f_bias, ml_norm_g, w_ml_out, gla_w_a2, gla_a_bias, gla_norm_g, w_gla_out, conf_dw_w, conf_dw_b, conf_ln_g, conf_ln_b, w_conf_out, conf_out_b, sc_dw_w, w_sc_out, merge_gate_b, w_o, ffn_w_up, ffn_dw_w, ffn_w_down):
    bsz = x.shape[0]
    depth = ada_w.shape[0]
    d = D_MODEL
    mod = _modulation(c, ada_w, ada_b)

    def row(a):
        return a.reshape(1, -1)

    for l in range(depth):
        m = mod[l].reshape(bsz, 6, 1, d)
        sh_t, sc_t, g_t, sh_c, sc_c, g_c = (m[:, i] for i in range(6))
        if_bias = jnp.pad(jnp.concatenate([ml_i_bias[l], ml_f_bias[l]]), (SMALL_I, 128 - SMALL_I - 2 * HEADS))[None]
        w_branch = jnp.stack([w_ml_out[l], w_gla_out[l], w_conf_out[l], w_sc_out[l]]).astype(BF16)
        x = _token_mixer(
            x, sh_t, sc_t, g_t, row(tm_pre_g[l]), row(tm_post_g[l]), _pack_w_in(w_in[l]),
            _pad_rows(if_bias, 8), _gate_spread_matrix(), _pad_rows(gla_w_a2[l], 128).astype(BF16), row(gla_a_bias[l]),
            row(ml_norm_g[l]), row(gla_norm_g[l]), w_branch, _pad_rows(conf_dw_w[l], CONF_HALO),
            row(conf_dw_b[l]), row(conf_ln_g[l]), row(conf_ln_b[l]), row(conf_out_b[l]),
            _pad_rows(sc_dw_w[l], 8), row(0.5 * merge_gate_b[l]), (0.5 * w_o[l]).astype(BF16))
        x = _channel_mixer(
            x, sh_c, sc_c, g_c, row(cm_pre_g[l]), row(cm_post_g[l]), ffn_w_up[l].astype(BF16),
            _pad_rows(ffn_dw_w[l], 8), ffn_w_down[l].astype(BF16))
    return x
```

```python
import jax
import jax.numpy as jnp
from jax import lax
from jax.experimental import pallas as pl
from jax.experimental.pallas import tpu as pltpu

D_MODEL = 1024
MIX = 256
HEADS = 4
HEAD_DIM = 64
CHUNK = 64
SUB = 16
GLA_TAU = 16.0
CONF_WIDTH = 31
CONF_HALO = 32
CONV3_HALO = 8
FFN_HIDDEN = 2816
EPS = 1e-6
Q_SCALE = HEAD_DIM ** -0.5

STREAMS = 1
SUB_TILE = 512
FFN_COLS = 256
GATE_PIECE = 256
DOWN_BLOCKS = 1
TM_LEAD = 1
FFN_LEAD = 3
V7X_VMEM_LIMIT_BYTES = 58 * 1024 * 1024

OFF_ML = 0
OFF_GLA = 1024
OFF_CONF = 2048
OFF_SC = 2560
OFF_SMALL = 3328
OFF_GATE = 3456
SMALL_I = 16
SMALL_F = 20
PACKED_COLS = OFF_GATE + 4 * D_MODEL

BF16 = jnp.bfloat16
F32 = jnp.float32


def _sigmoid(x):
    return 0.5 * (jnp.tanh(0.5 * x) + 1.0)


def _log_sigmoid(x):
    return jnp.minimum(x, 0.0) - jnp.log(1.0 + jnp.exp(-jnp.abs(x)))


def _dot(a, b):
    return jnp.dot(a, b, preferred_element_type=F32)


def _dot_nt(a, b):
    return lax.dot_general(a, b, (((1,), (1,)), ((), ())), preferred_element_type=F32)


def _dot_tn(a, b):
    return lax.dot_general(a, b, (((0,), (0,)), ((), ())), preferred_element_type=F32)


def _rms_norm(x, g):
    return x * lax.rsqrt(jnp.mean(x * x, axis=-1, keepdims=True) + EPS) * g


def _interleave(streams, lead):
    done = [False] * len(streams)
    turn = 0
    while not all(done):
        for i, stream in enumerate(streams):
            if not done[i] and turn >= i * lead:
                done[i] = next(stream, "end") == "end"
        turn += 1


def _mod_kernel(c_ref, w_ref, b_ref, o_ref):
    c = c_ref[...]
    c_act = (c * _sigmoid(c)).astype(BF16)
    o_ref[0] = _dot(c_act, w_ref[0].astype(BF16)) + b_ref[0]


def _modulation(c, ada_w, ada_b):
    depth, d, n = ada_w.shape
    bsz = c.shape[0]
    nb = 1536
    return pl.pallas_call(
        _mod_kernel,
        grid=(depth, n // nb),
        in_specs=[
            pl.BlockSpec((bsz, d), lambda l, j: (0, 0)),
            pl.BlockSpec((1, d, nb), lambda l, j: (l, 0, j)),
            pl.BlockSpec((1, 1, nb), lambda l, j: (l, 0, j)),
        ],
        out_specs=pl.BlockSpec((1, bsz, nb), lambda l, j: (l, 0, j)),
        out_shape=jax.ShapeDtypeStruct((depth, bsz, n), F32),
        compiler_params=pltpu.CompilerParams(
            dimension_semantics=("arbitrary", "arbitrary"), vmem_limit_bytes=V7X_VMEM_LIMIT_BYTES),
        name="adaln_modulation",
    )(c, ada_w, ada_b.reshape(depth, 1, n))


class _ColumnGroups:
    def __init__(self, groups):
        self._groups = groups

    def _locate(self, idx):
        rows, cols = idx
        for off, ref in self._groups:
            if off <= cols.start and cols.stop <= off + ref.shape[1]:
                return ref, (rows, slice(cols.start - off, cols.stop - off))
        raise IndexError(idx)

    def __getitem__(self, idx):
        ref, local = self._locate(idx)
        return ref[local]

    def __setitem__(self, idx, value):
        ref, local = self._locate(idx)
        ref[local] = value


def _token_mixer_kernel(
        x_ref, shift_ref, scale_ref, gate_ref, pre_g_ref, post_g_ref, w_in_ref, if_bias_ref, expand_ref,
        w_a2_ref, a_bias_ref, ml_norm_g_ref, gla_norm_g_ref, w_branch_ref, conf_dw_w_ref,
        conf_dw_b_ref, conf_ln_g_ref, conf_ln_b_ref, conf_out_b_ref, sc_dw_w_ref, merge_b_ref,
        w_o_ref, out_ref,
        proj_ml_ref, proj_gla_ref, proj_conf_ref, proj_sc_ref, proj_small_ref, gates_ref, branch_in_ref,
        ml_state_ref, ml_n_ref, ml_m_ref, gla_state_ref,
        conf_buf_ref, conv_part_ref, sc_buf_ref):
    tile = x_ref.shape[1]
    rows_n = tile // STREAMS

    @pl.when(pl.program_id(1) == 0)
    def _():
        ml_state_ref[...] = jnp.zeros_like(ml_state_ref)
        ml_n_ref[...] = jnp.zeros_like(ml_n_ref)
        ml_m_ref[...] = jnp.zeros_like(ml_m_ref)
        gla_state_ref[...] = jnp.zeros_like(gla_state_ref)
        conf_buf_ref[...] = jnp.zeros_like(conf_buf_ref)
        sc_buf_ref[0:CONV3_HALO, :] = jnp.zeros((CONV3_HALO, MIX), F32)

    proj_ref = _ColumnGroups(((OFF_ML, proj_ml_ref), (OFF_GLA, proj_gla_ref), (OFF_CONF, proj_conf_ref),
                              (OFF_SC, proj_sc_ref), (OFF_SMALL, proj_small_ref)))

    n_chunks = rows_n // CHUNK
    n_subs = CHUNK // SUB
    row = lax.broadcasted_iota(jnp.int32, (rows_n, MIX), 0)
    lane = lax.broadcasted_iota(jnp.int32, (rows_n, MIX), 1)
    pos = row & (CHUNK - 1)
    chunk_of_row = row >> 6
    sub_of_row = pos >> 4
    lane_head = lane >> 6
    lane_pos = lane & (HEAD_DIM - 1)
    causal = lane_pos <= pos
    eye = lane_pos == pos
    same_sub_causal = ((lane_pos >> 4) == sub_of_row) & causal
    last_of_chunk = pos == CHUNK - 1
    sq_row = lax.broadcasted_iota(jnp.int32, (MIX, MIX), 0)
    sq_lane = lax.broadcasted_iota(jnp.int32, (MIX, MIX), 1)
    block_diag = (sq_row >> 6) == (sq_lane >> 6)
    head_ones = jnp.where(block_diag, 1.0, 0.0).astype(BF16)
    lane128 = lax.broadcasted_iota(jnp.int32, (rows_n, 128), 1)
    lane_head_bf = lane_head.astype(BF16)
    head_lanes = [lane_head_bf == hd for hd in range(HEADS)]

    def seg_scan(a, op, fill):
        shift = 1
        while shift < CHUNK:
            a = op(a, jnp.where(pos >= shift, pltpu.roll(a, shift, 0), fill))
            shift *= 2
        return a

    def group_rows(a, keep, group):
        w = a.shape[1]
        return jnp.sum(jnp.where(keep, a, 0.0).reshape(rows_n // group, group, w), axis=1, keepdims=True)

    def spread_rows(g, group):
        w = g.shape[2]
        return jnp.broadcast_to(g, (rows_n // group, group, w)).reshape(rows_n, w)

    def chunk_rows(vals):
        out = vals[-1]
        for c in range(n_chunks - 2, -1, -1):
            out = jnp.where(chunk_of_row[:, 0:vals[c].shape[1]] == c, vals[c], out)
        return out

    def expand_heads(a):
        reps = a.shape[1] // MIX
        a_bf = a.astype(BF16)
        zero = jnp.zeros_like(a_bf)
        parts = []
        for hd in range(HEADS):
            keep = head_lanes[hd] if reps == 1 else jnp.concatenate([head_lanes[hd]] * reps, axis=1)
            parts.append(jnp.where(keep, a_bf, zero))
        return [jnp.concatenate([p[c * CHUNK:(c + 1) * CHUNK] for p in parts], axis=0)
                for c in range(n_chunks)]

    def per_chunk(fn):
        return jnp.concatenate([fn(c, slice(c * CHUNK, (c + 1) * CHUNK)) for c in range(n_chunks)], axis=0)

    def head_rms_norm(t, g):
        ms = _dot((t * t).astype(BF16), head_ones) * (1.0 / HEAD_DIM)
        return t * lax.rsqrt(ms + EPS) * g

    def sub_tile_program(s):
        r0 = s * rows_n
        rows = slice(r0, r0 + rows_n)
        is_last = s == STREAMS - 1

        x = x_ref[0, rows, :]
        h_bf = (_rms_norm(x, pre_g_ref[...] * (1.0 + scale_ref[0])) + shift_ref[0]).astype(BF16)
        yield

        def project(col, width):
            proj_ref[rows, col:col + width] = _dot(h_bf, w_in_ref[:, col:col + width])

        def conv_group(r):
            acc = None
            for a in range(CONF_HALO // 8 + 1):
                j = 8 * a + r - (CONF_HALO - CONF_WIDTH + 1)
                if 0 <= j < CONF_WIDTH:
                    term = conf_dw_w_ref[j:j + 1, :] * conf_buf_ref[r0 + 8 * a:r0 + 8 * a + rows_n + 8, :]
                    acc = term if acc is None else acc + term
            if r == 0:
                return acc[0:rows_n, :]
            slot = s * 7 + r - 1
            conv_part_ref[slot] = acc
            return conv_part_ref[slot, r:r + rows_n, :]

        project(OFF_CONF, 2 * MIX)
        glu = proj_ref[rows, OFF_CONF:OFF_CONF + MIX] * _sigmoid(proj_ref[rows, OFF_CONF + MIX:OFF_CONF + 2 * MIX])
        conf_buf_ref[CONF_HALO + r0:CONF_HALO + r0 + rows_n, :] = glu

        project(OFF_SC, 3 * MIX)
        conv = conf_dw_b_ref[...] + conv_group(0)
        conv = conv + conv_group(1)

        project(OFF_ML, 4 * MIX)
        conv = conv + conv_group(2)
        conv = conv + conv_group(3)
        conv = conv + conv_group(4)

        project(OFF_SMALL, 128)
        conv = conv + conv_group(5)
        conv = conv + conv_group(6)

        project(OFF_GLA, 4 * MIX)
        conv = conv + conv_group(7)
        if is_last:
            conf_buf_ref[0:CONF_HALO, :] = conf_buf_ref[tile:tile + CONF_HALO, :]
        mu = jnp.mean(conv, axis=-1, keepdims=True)
        cen = conv - mu
        ln = cen * lax.rsqrt(jnp.mean(cen * cen, axis=-1, keepdims=True) + EPS) * conf_ln_g_ref[...] \
            + conf_ln_b_ref[...]
        branch_in_ref[2, rows, :] = (ln * _sigmoid(ln)).astype(BF16)

        cx = proj_ref[rows, OFF_SC + MIX:OFF_SC + 2 * MIX] * proj_ref[rows, OFF_SC + 2 * MIX:OFF_SC + 3 * MIX]
        sc0 = CONV3_HALO + r0
        sc_buf_ref[sc0:sc0 + rows_n, :] = cx
        sconv = (sc_dw_w_ref[0:1, :] * sc_buf_ref[sc0 - 2:sc0 - 2 + rows_n, :]
                 + sc_dw_w_ref[1:2, :] * sc_buf_ref[sc0 - 1:sc0 - 1 + rows_n, :]
                 + sc_dw_w_ref[2:3, :] * cx)
        if is_last:
            sc_buf_ref[0:CONV3_HALO, :] = sc_buf_ref[tile:tile + CONV3_HALO, :]
        branch_in_ref[3, rows, :] = (proj_ref[rows, OFF_SC:OFF_SC + MIX] * sconv).astype(BF16)
        yield

        gate_cols = iter(range(0, 4 * D_MODEL, GATE_PIECE))

        def next_gate():
            col = next(gate_cols, None)
            if col is not None:
                z = _dot(h_bf, w_in_ref[:, OFF_GATE + col:OFF_GATE + col + GATE_PIECE])
                gates_ref[rows, col:col + GATE_PIECE] = jnp.tanh(z + merge_b_ref[:, col:col + GATE_PIECE]) + 1.0

        next_gate()
        q = proj_ref[rows, OFF_ML:OFF_ML + MIX] * Q_SCALE
        q_bf = q.astype(BF16)
        k = proj_ref[rows, OFF_ML + MIX:OFF_ML + 2 * MIX]
        v = proj_ref[rows, OFF_ML + 2 * MIX:OFF_ML + 3 * MIX]
        small = proj_ref[rows, OFF_SMALL:OFF_SMALL + 128]
        small_pre = small + if_bias_ref[0:1, :]
        compact = jnp.where((lane128 >= SMALL_F) & (lane128 < SMALL_F + HEADS), _log_sigmoid(small_pre), small_pre)
        compact_hi = compact.astype(BF16)
        compact_lo = (compact - compact_hi.astype(F32)).astype(BF16)
        spread = _dot(jnp.concatenate([compact_hi, compact_lo], axis=1), expand_ref[...])
        i_pre = spread[:, 0:MIX]
        log_f = spread[:, MIX:2 * MIX]
        next_gate()
        b = seg_scan(log_f, jnp.add, 0.0)
        u = i_pre - b
        next_gate()
        u_max = seg_scan(u, jnp.maximum, -jnp.inf)
        next_gate()

        gq = proj_ref[rows, OFF_GLA:OFF_GLA + MIX] * Q_SCALE
        gk = proj_ref[rows, OFF_GLA + MIX:OFF_GLA + 2 * MIX]
        gv = proj_ref[rows, OFF_GLA + 2 * MIX:OFF_GLA + 3 * MIX]
        low = small.astype(BF16)
        log_a = _log_sigmoid(_dot(low, w_a2_ref[...]) + a_bias_ref[...]) * (1.0 / GLA_TAU)
        next_gate()
        gb = seg_scan(log_a, jnp.add, 0.0)
        next_gate()
        sub_ends = group_rows(gb, (pos & (SUB - 1)) == SUB - 1, SUB)
        sub_end = spread_rows(sub_ends, SUB)
        sub_id = lax.broadcasted_iota(jnp.int32, (rows_n // SUB, 1, MIX), 0)
        prev_ends = jnp.concatenate([jnp.zeros((1, 1, MIX), F32), sub_ends[:-1]], axis=0)
        sub_start = spread_rows(jnp.where((sub_id & (n_subs - 1)) == 0, 0.0, prev_ends), SUB)
        ends_by_chunk = sub_ends.reshape(n_chunks, n_subs, MIX)
        next_gate()
        yield

        chunk_max = group_rows(u_max, last_of_chunk, CHUNK)
        chunk_b = group_rows(b, last_of_chunk, CHUNK)
        m_start = [ml_m_ref[...]]
        m_end = []
        for c in range(n_chunks):
            m_end.append(jnp.maximum(m_start[c], chunk_max[c]))
            m_start.append(chunk_b[c] + m_end[c])
        ml_m_ref[...] = m_start[n_chunks]
        m_in = chunk_rows(m_start[:n_chunks])
        m_run = jnp.maximum(m_in, u_max)
        k_exp = expand_heads(k)
        raw_scores = per_chunk(lambda c, cr: _dot_nt(q_bf[cr], k_exp[c]))
        next_gate()
        u_lane = spread_rows(group_rows(u, eye, CHUNK), CHUNK)
        decay_mat = jnp.exp(jnp.where(causal, u_lane - m_run, -jnp.inf))

        q_loc = (gq * jnp.exp(gb - sub_start)).astype(BF16)
        k_loc_exp = expand_heads(gk * jnp.exp(sub_start - gb))
        attn = jnp.where(same_sub_causal, per_chunk(lambda c, cr: _dot_nt(q_loc[cr], k_loc_exp[c])), 0.0)
        next_gate()

        scores_bf = (raw_scores * decay_mat).astype(BF16)
        v_exp = expand_heads(v)
        intra = per_chunk(lambda c, cr: _dot(scores_bf[cr], v_exp[c]))
        den_intra = _dot(scores_bf, head_ones)
        next_gate()

        k_end = gk * jnp.exp(sub_end - gb)
        for j in range(n_subs - 1):
            end_j = chunk_rows([ends_by_chunk[c, j:j + 1, :] for c in range(n_chunks)])
            q_j = jnp.where(sub_of_row > j, gq * jnp.exp(jnp.minimum(gb - end_j, 0.0)), 0.0).astype(BF16)
            k_j_exp = expand_heads(jnp.where(sub_of_row == j, k_end, 0.0))
            attn = attn + per_chunk(lambda c, cr: _dot_nt(q_j[cr], k_j_exp[c]))
            next_gate()
        yield

        k_w = k * jnp.exp(u - chunk_rows(m_end))
        k_w_bf = k_w.astype(BF16)
        k_w_sums = jnp.sum(k_w.reshape(n_chunks, CHUNK, MIX), axis=1, keepdims=True)
        v_bf = v.astype(BF16)
        state = ml_state_ref[...]
        norm_vec = ml_n_ref[...]
        inter_parts = []
        norm_in = []
        for c in range(n_chunks):
            cr = slice(c * CHUNK, (c + 1) * CHUNK)
            inter_parts.append(_dot(q_bf[cr], state.astype(BF16)))
            norm_in.append(norm_vec)
            decay = jnp.exp(m_start[c] - m_end[c])
            state = state * decay + jnp.where(block_diag, _dot_tn(k_w_bf[cr], v_bf[cr]), 0.0)
            norm_vec = norm_vec * decay + k_w_sums[c]
        ml_state_ref[...] = state
        ml_n_ref[...] = norm_vec
        den_inter = _dot((q * chunk_rows(norm_in)).astype(BF16), head_ones)
        next_gate()

        attn_bf = attn.astype(BF16)
        gv_exp = expand_heads(gv)
        g_intra = per_chunk(lambda c, cr: _dot(attn_bf[cr], gv_exp[c]))
        next_gate()

        inter = jnp.concatenate(inter_parts, axis=0)
        w_inter = jnp.exp(m_in - m_run)
        num = w_inter * inter + intra
        den = w_inter * den_inter + den_intra
        ml_h = num / jnp.maximum(jnp.abs(den), jnp.exp(-(m_run + b)))

        chunk_ends = [ends_by_chunk[c, n_subs - 1:n_subs, :] for c in range(n_chunks)]
        q_dec = (gq * jnp.exp(gb)).astype(BF16)
        k_dec = (gk * jnp.exp(chunk_rows(chunk_ends) - gb)).astype(BF16)
        gv_bf = gv.astype(BF16)
        g_state = gla_state_ref[...]
        g_inter_parts = []
        for c in range(n_chunks):
            cr = slice(c * CHUNK, (c + 1) * CHUNK)
            g_inter_parts.append(_dot_nt(q_dec[cr], g_state.astype(BF16)))
            g_state = (g_state * jnp.exp(chunk_ends[c])
                       + jnp.where(block_diag, _dot_tn(gv_bf[cr], k_dec[cr]), 0.0))
        gla_state_ref[...] = g_state
        next_gate()
        gla_h = jnp.concatenate(g_inter_parts, axis=0) + g_intra
        assert next(gate_cols, None) is None
        yield

        ml = head_rms_norm(ml_h, ml_norm_g_ref[...])
        ml_in = (ml * _sigmoid(proj_ref[rows, OFF_ML + 3 * MIX:OFF_ML + 4 * MIX])).astype(BF16)
        gl = head_rms_norm(gla_h, gla_norm_g_ref[...])
        r = proj_ref[rows, OFF_GLA + 3 * MIX:OFF_GLA + 4 * MIX]
        gl_in = (gl * (r * _sigmoid(r))).astype(BF16)

        merged = gates_ref[rows, 0:D_MODEL] * _dot(ml_in, w_branch_ref[0])
        merged = merged + gates_ref[rows, D_MODEL:2 * D_MODEL] * _dot(gl_in, w_branch_ref[1])
        merged = merged + gates_ref[rows, 2 * D_MODEL:3 * D_MODEL] * (
            _dot(branch_in_ref[2, rows, :], w_branch_ref[2]) + conf_out_b_ref[...])
        merged = merged + gates_ref[rows, 3 * D_MODEL:4 * D_MODEL] * _dot(branch_in_ref[3, rows, :], w_branch_ref[3])
        merged_bf = merged.astype(BF16)
        yield

        y = _dot(merged_bf, w_o_ref[...])
        yield

        out_ref[0, rows, :] = x + _rms_norm(y, post_g_ref[...] * gate_ref[0])

    _interleave([sub_tile_program(s) for s in range(STREAMS)], TM_LEAD)


def _resident(shape):
    zeros = (0,) * len(shape)
    return pl.BlockSpec(shape, lambda b, t: zeros, pipeline_mode=pl.Buffered(1))


def _token_mixer(x, shift, scale, gate, pre_g, post_g, w_in, if_bias, expand, w_a2, a_bias, ml_norm_g,
                 gla_norm_g, w_branch, conf_dw_w, conf_dw_b, conf_ln_g, conf_ln_b, conf_out_b,
                 sc_dw_w, merge_b, w_o):
    bsz, seq, d = x.shape
    tile = STREAMS * SUB_TILE
    assert seq % tile == 0 and SUB_TILE % CHUNK == 0 and d == D_MODEL
    x_spec = pl.BlockSpec((1, tile, d), lambda b, t: (b, t, 0))
    mod_spec = pl.BlockSpec((1, 1, d), lambda b, t: (b, 0, 0))
    params = (pre_g, post_g, w_in, if_bias, expand, w_a2, a_bias, ml_norm_g, gla_norm_g, w_branch,
              conf_dw_w, conf_dw_b, conf_ln_g, conf_ln_b, conf_out_b, sc_dw_w, merge_b, w_o)
    return pl.pallas_call(
        _token_mixer_kernel,
        grid=(bsz, seq // tile),
        in_specs=[x_spec, mod_spec, mod_spec, mod_spec] + [_resident(p.shape) for p in params],
        out_specs=x_spec,
        out_shape=jax.ShapeDtypeStruct(x.shape, x.dtype),
        scratch_shapes=[
            pltpu.VMEM((tile, 4 * MIX), F32),
            pltpu.VMEM((tile, 4 * MIX), F32),
            pltpu.VMEM((tile, 2 * MIX), F32),
            pltpu.VMEM((tile, 3 * MIX), F32),
            pltpu.VMEM((tile, 128), F32),
            pltpu.VMEM((tile, 4 * D_MODEL), F32),
            pltpu.VMEM((4, tile, MIX), BF16),
            pltpu.VMEM((MIX, MIX), F32),
            pltpu.VMEM((1, MIX), F32),
            pltpu.VMEM((1, MIX), F32),
            pltpu.VMEM((MIX, MIX), F32),
            pltpu.VMEM((CONF_HALO + tile + 8, MIX), F32),
            pltpu.VMEM((STREAMS * 7, SUB_TILE + 8, MIX), F32),
            pltpu.VMEM((CONV3_HALO + tile, MIX), F32),
        ],
        compiler_params=pltpu.CompilerParams(
            dimension_semantics=("arbitrary", "arbitrary"), vmem_limit_bytes=V7X_VMEM_LIMIT_BYTES),
        name="token_mixer",
    )(x, shift, scale, gate, *params)


def _channel_mixer_kernel(x_ref, shift_ref, scale_ref, gate_ref, pre_g_ref, post_g_ref, w_up_ref,
                          dw_w_ref, w_down_ref, out_ref, u_buf_ref, act_ref):
    tile = x_ref.shape[1]
    rows_n = tile // STREAMS
    n_blocks = FFN_HIDDEN // FFN_COLS

    @pl.when(pl.program_id(1) == 0)
    def _():
        u_buf_ref[0:CONV3_HALO, :] = jnp.zeros((CONV3_HALO, 2 * FFN_HIDDEN), F32)

    def sub_tile_program(s):
        r0 = s * rows_n
        rows = slice(r0, r0 + rows_n)
        u0 = CONV3_HALO + r0

        x = x_ref[0, rows, :]
        h_bf = (_rms_norm(x, pre_g_ref[...] * (1.0 + scale_ref[0])) + shift_ref[0]).astype(BF16)
        yield

        def conv3(cols):
            return (dw_w_ref[0:1, cols] * u_buf_ref[u0 - 2:u0 - 2 + rows_n, cols]
                    + dw_w_ref[1:2, cols] * u_buf_ref[u0 - 1:u0 - 1 + rows_n, cols]
                    + dw_w_ref[2:3, cols] * u_buf_ref[u0:u0 + rows_n, cols])

        for n in range(n_blocks):
            cols_a = slice(n * FFN_COLS, (n + 1) * FFN_COLS)
            cols_v = slice(FFN_HIDDEN + n * FFN_COLS, FFN_HIDDEN + (n + 1) * FFN_COLS)
            u_buf_ref[u0:u0 + rows_n, cols_a] = _dot(h_bf, w_up_ref[:, cols_a])
            u_buf_ref[u0:u0 + rows_n, cols_v] = _dot(h_bf, w_up_ref[:, cols_v])
            yield
            a = conv3(cols_a)
            v = conv3(cols_v)
            act_ref[rows, cols_a] = (a * _sigmoid(a) * v).astype(BF16)
            if s == STREAMS - 1:
                for cols in (cols_a, cols_v):
                    u_buf_ref[0:CONV3_HALO, cols] = u_buf_ref[tile:tile + CONV3_HALO, cols]
            yield

        y = None
        for n in range(0, n_blocks, DOWN_BLOCKS):
            k_rows = slice(n * FFN_COLS, min(n + DOWN_BLOCKS, n_blocks) * FFN_COLS)
            part = _dot(act_ref[rows, k_rows], w_down_ref[k_rows, :])
            y = part if y is None else y + part
            yield

        out_ref[0, rows, :] = x + _rms_norm(y, post_g_ref[...] * gate_ref[0])

    _interleave([sub_tile_program(s) for s in range(STREAMS)], FFN_LEAD)


def _channel_mixer(x, shift, scale, gate, pre_g, post_g, w_up, dw_w, w_down):
    bsz, seq, d = x.shape
    tile = STREAMS * SUB_TILE
    assert seq % tile == 0 and d == D_MODEL
    x_spec = pl.BlockSpec((1, tile, d), lambda b, t: (b, t, 0))
    mod_spec = pl.BlockSpec((1, 1, d), lambda b, t: (b, 0, 0))
    params = (pre_g, post_g, w_up, dw_w, w_down)
    return pl.pallas_call(
        _channel_mixer_kernel,
        grid=(bsz, seq // tile),
        in_specs=[x_spec, mod_spec, mod_spec, mod_spec] + [_resident(p.shape) for p in params],
        out_specs=x_spec,
        out_shape=jax.ShapeDtypeStruct(x.shape, x.dtype),
        scratch_shapes=[
            pltpu.VMEM((CONV3_HALO + tile, 2 * FFN_HIDDEN), F32),
            pltpu.VMEM((tile, FFN_HIDDEN), BF16),
        ],
        compiler_params=pltpu.CompilerParams(
            dimension_semantics=("arbitrary", "arbitrary"), vmem_limit_bytes=V7X_VMEM_LIMIT_BYTES),
        name="channel_mixer",
    )(x, shift, scale, gate, *params)


def _pad_rows(a, n):
    return jnp.pad(a, ((0, n - a.shape[0]), (0, 0)))


def _pack_w_in(w):
    ml = w[:, 0:1024]
    ml_i = w[:, 1024:1028]
    ml_f = w[:, 1028:1032]
    gla = w[:, 1032:2056]
    gla_a = w[:, 2056:2072]
    conf = w[:, 2072:2584]
    sc = w[:, 2584:3352]
    gate = w[:, 3352:7448]
    small = jnp.concatenate([gla_a, ml_i, ml_f], axis=1)
    assert gla_a.shape[1] == SMALL_I and SMALL_F == SMALL_I + HEADS
    packed = jnp.concatenate([
        ml, gla, conf, sc, jnp.pad(small, ((0, 0), (0, 128 - small.shape[1]))), 0.5 * gate], axis=1)
    assert packed.shape[1] == PACKED_COLS
    return packed.astype(BF16)


def _gate_spread_matrix():
    lane = jnp.arange(128)[:, None]
    col = jnp.arange(2 * MIX)[None, :]
    is_forget = col >= MIX
    head = (col % MIX) // HEAD_DIM
    one = lane == jnp.where(is_forget, SMALL_F, SMALL_I) + head
    half = jnp.where(one, 1.0, 0.0).astype(BF16)
    return jnp.concatenate([half, half], axis=0)


def kernel(x, c, ada_w, ada_b, tm_pre_g, tm_post_g, cm_pre_g, cm_post_g, w_in, ml_i_bias, ml_f_bias, ml_norm_g, w_ml_out, gla_w_a2, gla_a_bias, gla_norm_g, w_gla_out, conf_dw_w, conf_dw_b, conf_ln_g, conf_ln_b, w_conf_out, conf_out_b, sc_dw_w, w_sc_out, merge_gate_b, w_o, ffn_w_up, ffn_dw_w, ffn_w_down):
    bsz = x.shape[0]
    depth = ada_w.shape[0]
    d = D_MODEL
    mod = _modulation(c, ada_w, ada_b)

    def row(a):
        return a.reshape(1, -1)

    for l in range(depth):
        m = mod[l].reshape(bsz, 6, 1, d)
        sh_t, sc_t, g_t, sh_c, sc_c, g_c = (m[:, i] for i in range(6))
        if_bias = jnp.pad(jnp.concatenate([ml_i_bias[l], ml_f_bias[l]]), (SMALL_I, 128 - SMALL_I - 2 * HEADS))[None]
        w_branch = jnp.stack([w_ml_out[l], w_gla_out[l], w_conf_out[l], w_sc_out[l]]).astype(BF16)
        x = _token_mixer(
            x, sh_t, sc_t, g_t, row(tm_pre_g[l]), row(tm_post_g[l]), _pack_w_in(w_in[l]),
            _pad_rows(if_bias, 8), _gate_spread_matrix(), _pad_rows(gla_w_a2[l], 128).astype(BF16), row(gla_a_bias[l]),
            row(ml_norm_g[l]), row(gla_norm_g[l]), w_branch, _pad_rows(conf_dw_w[l], CONF_HALO),
            row(conf_dw_b[l]), row(conf_ln_g[l]), row(conf_ln_b[l]), row(conf_out_b[l]),
            _pad_rows(sc_dw_w[l], 8), row(0.5 * merge_gate_b[l]), (0.5 * w_o[l]).astype(BF16))
        x = _channel_mixer(
            x, sh_c, sc_c, g_c, row(cm_pre_g[l]), row(cm_post_g[l]), ffn_w_up[l].astype(BF16),
            _pad_rows(ffn_dw_w[l], 8), ffn_w_down[l].astype(BF16))
    return x
```

```python
import functools

import jax
import jax.numpy as jnp
from jax import lax
from jax.experimental import pallas as pl
from jax.experimental.pallas import tpu as pltpu

D_MODEL = 1024
MIX = 256
HEADS = 4
HEAD_DIM = 64
CHUNK = 64
SUB = 16
GLA_TAU = 16.0
CONF_WIDTH = 31
CONF_HALO = 32
CONV3_HALO = 8
FFN_HIDDEN = 2816
EPS = 1e-6
Q_SCALE = HEAD_DIM ** -0.5

STREAMS = 1
SUB_TILE = 512
FFN_COLS = 256
GATE_PIECE = 256
DOWN_BLOCKS = 1
TM_LEAD = 1
FFN_LEAD = 3
V7X_VMEM_LIMIT_BYTES = 58 * 1024 * 1024

OFF_ML = 0
OFF_GLA = 1024
OFF_CONF = 2048
OFF_SC = 2560
OFF_SMALL = 3328
OFF_GATE = 3456
SMALL_I = 16
SMALL_F = 20
PACKED_COLS = OFF_GATE + 4 * D_MODEL

BF16 = jnp.bfloat16
F32 = jnp.float32


def _sigmoid(x):
    return 0.5 * (jnp.tanh(0.5 * x) + 1.0)


def _log_sigmoid(x):
    return jnp.minimum(x, 0.0) - jnp.log(1.0 + jnp.exp(-jnp.abs(x)))


def _dot(a, b):
    return jnp.dot(a, b, preferred_element_type=F32)


def _dot_nt(a, b):
    return lax.dot_general(a, b, (((1,), (1,)), ((), ())), preferred_element_type=F32)


def _dot_tn(a, b):
    return lax.dot_general(a, b, (((0,), (0,)), ((), ())), preferred_element_type=F32)


def _rms_norm(x, g):
    return x * lax.rsqrt(jnp.mean(x * x, axis=-1, keepdims=True) + EPS) * g


def _interleave(streams, lead):
    done = [False] * len(streams)
    turn = 0
    while not all(done):
        for i, stream in enumerate(streams):
            if not done[i] and turn >= i * lead:
                done[i] = next(stream, "end") == "end"
        turn += 1


def _mod_kernel(c_ref, w_ref, b_ref, o_ref):
    c = c_ref[...]
    c_act = (c * _sigmoid(c)).astype(BF16)
    o_ref[0] = _dot(c_act, w_ref[0].astype(BF16)) + b_ref[0]


def _modulation(c, ada_w, ada_b):
    depth, d, n = ada_w.shape
    bsz = c.shape[0]
    nb = 1536
    return pl.pallas_call(
        _mod_kernel,
        grid=(depth, n // nb),
        in_specs=[
            pl.BlockSpec((bsz, d), lambda l, j: (0, 0)),
            pl.BlockSpec((1, d, nb), lambda l, j: (l, 0, j)),
            pl.BlockSpec((1, 1, nb), lambda l, j: (l, 0, j)),
        ],
        out_specs=pl.BlockSpec((1, bsz, nb), lambda l, j: (l, 0, j)),
        out_shape=jax.ShapeDtypeStruct((depth, bsz, n), F32),
        compiler_params=pltpu.CompilerParams(
            dimension_semantics=("arbitrary", "arbitrary"), vmem_limit_bytes=V7X_VMEM_LIMIT_BYTES),
        name="adaln_modulation",
    )(c, ada_w, ada_b.reshape(depth, 1, n))


class _ColumnGroups:
    def __init__(self, groups):
        self._groups = groups

    def _locate(self, idx):
        rows, cols = idx
        for off, ref in self._groups:
            if off <= cols.start and cols.stop <= off + ref.shape[1]:
                return ref, (rows, slice(cols.start - off, cols.stop - off))
        raise IndexError(idx)

    def __getitem__(self, idx):
        ref, local = self._locate(idx)
        return ref[local]

    def __setitem__(self, idx, value):
        ref, local = self._locate(idx)
        ref[local] = value


def _token_mixer_kernel(
        x_ref, shift_ref, scale_ref, gate_ref, pre_g_ref, post_g_ref, w_in_ref, if_bias_ref, expand_ref,
        w_a2_ref, a_bias_ref, ml_norm_g_ref, gla_norm_g_ref, w_branch_ref, conf_dw_w_ref,
        conf_dw_b_ref, conf_ln_g_ref, conf_ln_b_ref, conf_out_b_ref, sc_dw_w_ref, merge_b_ref,
        w_o_ref, out_ref,
        proj_ml_ref, proj_gla_ref, proj_conf_ref, proj_sc_ref, proj_small_ref, gates_ref, branch_in_ref,
        ml_state_ref, ml_n_ref, ml_m_ref, gla_state_ref,
        conf_buf_ref, conv_part_ref, sc_buf_ref):
    tile = x_ref.shape[1]
    rows_n = tile // STREAMS

    @pl.when(pl.program_id(1) == 0)
    def _():
        ml_state_ref[...] = jnp.zeros_like(ml_state_ref)
        ml_n_ref[...] = jnp.zeros_like(ml_n_ref)
        ml_m_ref[...] = jnp.zeros_like(ml_m_ref)
        gla_state_ref[...] = jnp.zeros_like(gla_state_ref)
        conf_buf_ref[...] = jnp.zeros_like(conf_buf_ref)
        sc_buf_ref[0:CONV3_HALO, :] = jnp.zeros((CONV3_HALO, MIX), F32)

    proj_ref = _ColumnGroups(((OFF_ML, proj_ml_ref), (OFF_GLA, proj_gla_ref), (OFF_CONF, proj_conf_ref),
                              (OFF_SC, proj_sc_ref), (OFF_SMALL, proj_small_ref)))

    n_chunks = rows_n // CHUNK
    n_subs = CHUNK // SUB
    row = lax.broadcasted_iota(jnp.int32, (rows_n, MIX), 0)
    lane = lax.broadcasted_iota(jnp.int32, (rows_n, MIX), 1)
    pos = row & (CHUNK - 1)
    chunk_of_row = row >> 6
    sub_of_row = pos >> 4
    lane_head = lane >> 6
    lane_pos = lane & (HEAD_DIM - 1)
    causal = lane_pos <= pos
    eye = lane_pos == pos
    same_sub_causal = ((lane_pos >> 4) == sub_of_row) & causal
    last_of_chunk = pos == CHUNK - 1
    sq_row = lax.broadcasted_iota(jnp.int32, (MIX, MIX), 0)
    sq_lane = lax.broadcasted_iota(jnp.int32, (MIX, MIX), 1)
    block_diag = (sq_row >> 6) == (sq_lane >> 6)
    head_ones = jnp.where(block_diag, 1.0, 0.0).astype(BF16)
    lane128 = lax.broadcasted_iota(jnp.int32, (rows_n, 128), 1)
    lane_head_bf = lane_head.astype(BF16)
    head_lanes = [lane_head_bf == hd for hd in range(HEADS)]

    def seg_scan(a, op, fill):
        shift = 1
        while shift < CHUNK:
            a = op(a, jnp.where(pos >= shift, pltpu.roll(a, shift, 0), fill))
            shift *= 2
        return a

    def group_rows(a, keep, group):
        w = a.shape[1]
        return jnp.sum(jnp.where(keep, a, 0.0).reshape(rows_n // group, group, w), axis=1, keepdims=True)

    def spread_rows(g, group):
        w = g.shape[2]
        return jnp.broadcast_to(g, (rows_n // group, group, w)).reshape(rows_n, w)

    def chunk_rows(vals):
        out = vals[-1]
        for c in range(n_chunks - 2, -1, -1):
            out = jnp.where(chunk_of_row[:, 0:vals[c].shape[1]] == c, vals[c], out)
        return out

    def expand_heads(a):
        reps = a.shape[1] // MIX
        a_bf = a.astype(BF16)
        zero = jnp.zeros_like(a_bf)
        parts = []
        for hd in range(HEADS):
            keep = head_lanes[hd] if reps == 1 else jnp.concatenate([head_lanes[hd]] * reps, axis=1)
            parts.append(jnp.where(keep, a_bf, zero))
        return [jnp.concatenate([p[c * CHUNK:(c + 1) * CHUNK] for p in parts], axis=0)
                for c in range(n_chunks)]

    def per_chunk(fn):
        return jnp.concatenate([fn(c, slice(c * CHUNK, (c + 1) * CHUNK)) for c in range(n_chunks)], axis=0)

    def head_rms_norm(t, g):
        ms = _dot((t * t).astype(BF16), head_ones) * (1.0 / HEAD_DIM)
        return t * lax.rsqrt(ms + EPS) * g

    def sub_tile_program(s):
        r0 = s * rows_n
        rows = slice(r0, r0 + rows_n)
        is_last = s == STREAMS - 1

        x = x_ref[0, rows, :]
        h_bf = (_rms_norm(x, pre_g_ref[...] * (1.0 + scale_ref[0])) + shift_ref[0]).astype(BF16)
        yield

        def project(col, width):
            proj_ref[rows, col:col + width] = _dot(h_bf, w_in_ref[:, col:col + width])

        def conv_group(r):
            acc = None
            for a in range(CONF_HALO // 8 + 1):
                j = 8 * a + r - (CONF_HALO - CONF_WIDTH + 1)
                if 0 <= j < CONF_WIDTH:
                    term = conf_dw_w_ref[j:j + 1, :] * conf_buf_ref[r0 + 8 * a:r0 + 8 * a + rows_n + 8, :]
                    acc = term if acc is None else acc + term
            if r == 0:
                return acc[0:rows_n, :]
            slot = s * 7 + r - 1
            conv_part_ref[slot] = acc
            return conv_part_ref[slot, r:r + rows_n, :]

        project(OFF_CONF, 2 * MIX)
        glu = proj_ref[rows, OFF_CONF:OFF_CONF + MIX] * _sigmoid(proj_ref[rows, OFF_CONF + MIX:OFF_CONF + 2 * MIX])
        conf_buf_ref[CONF_HALO + r0:CONF_HALO + r0 + rows_n, :] = glu

        project(OFF_SC, 3 * MIX)
        conv = conf_dw_b_ref[...] + conv_group(0)
        conv = conv + conv_group(1)

        project(OFF_ML, 4 * MIX)
        conv = conv + conv_group(2)
        conv = conv + conv_group(3)
        conv = conv + conv_group(4)

        project(OFF_SMALL, 128)
        conv = conv + conv_group(5)
        conv = conv + conv_group(6)

        project(OFF_GLA, 4 * MIX)
        conv = conv + conv_group(7)
        if is_last:
            conf_buf_ref[0:CONF_HALO, :] = conf_buf_ref[tile:tile + CONF_HALO, :]
        mu = jnp.mean(conv, axis=-1, keepdims=True)
        cen = conv - mu
        ln = cen * lax.rsqrt(jnp.mean(cen * cen, axis=-1, keepdims=True) + EPS) * conf_ln_g_ref[...] \
            + conf_ln_b_ref[...]
        branch_in_ref[2, rows, :] = (ln * _sigmoid(ln)).astype(BF16)

        cx = proj_ref[rows, OFF_SC + MIX:OFF_SC + 2 * MIX] * proj_ref[rows, OFF_SC + 2 * MIX:OFF_SC + 3 * MIX]
        sc0 = CONV3_HALO + r0
        sc_buf_ref[sc0:sc0 + rows_n, :] = cx
        sconv = (sc_dw_w_ref[0:1, :] * sc_buf_ref[sc0 - 2:sc0 - 2 + rows_n, :]
                 + sc_dw_w_ref[1:2, :] * sc_buf_ref[sc0 - 1:sc0 - 1 + rows_n, :]
                 + sc_dw_w_ref[2:3, :] * cx)
        if is_last:
            sc_buf_ref[0:CONV3_HALO, :] = sc_buf_ref[tile:tile + CONV3_HALO, :]
        branch_in_ref[3, rows, :] = (proj_ref[rows, OFF_SC:OFF_SC + MIX] * sconv).astype(BF16)
        yield

        gate_cols = iter(range(0, 4 * D_MODEL, GATE_PIECE))

        def next_gate():
            col = next(gate_cols, None)
            if col is not None:
                z = _dot(h_bf, w_in_ref[:, OFF_GATE + col:OFF_GATE + col + GATE_PIECE])
                gates_ref[rows, col:col + GATE_PIECE] = jnp.tanh(z + merge_b_ref[:, col:col + GATE_PIECE]) + 1.0

        next_gate()
        q = proj_ref[rows, OFF_ML:OFF_ML + MIX] * Q_SCALE
        q_bf = q.astype(BF16)
        k = proj_ref[rows, OFF_ML + MIX:OFF_ML + 2 * MIX]
        v = proj_ref[rows, OFF_ML + 2 * MIX:OFF_ML + 3 * MIX]
        small = proj_ref[rows, OFF_SMALL:OFF_SMALL + 128]
        small_pre = small + if_bias_ref[0:1, :]
        compact = jnp.where((lane128 >= SMALL_F) & (lane128 < SMALL_F + HEADS), _log_sigmoid(small_pre), small_pre)
        compact_hi = compact.astype(BF16)
        compact_lo = (compact - compact_hi.astype(F32)).astype(BF16)
        spread = _dot(jnp.concatenate([compact_hi, compact_lo], axis=1), expand_ref[...])
        i_pre = spread[:, 0:MIX]
        log_f = spread[:, MIX:2 * MIX]
        next_gate()
        b = seg_scan(log_f, jnp.add, 0.0)
        u = i_pre - b
        next_gate()
        u_max = seg_scan(u, jnp.maximum, -jnp.inf)
        next_gate()

        gq = proj_ref[rows, OFF_GLA:OFF_GLA + MIX] * Q_SCALE
        gk = proj_ref[rows, OFF_GLA + MIX:OFF_GLA + 2 * MIX]
        gv = proj_ref[rows, OFF_GLA + 2 * MIX:OFF_GLA + 3 * MIX]
        low = small.astype(BF16)
        log_a = _log_sigmoid(_dot(low, w_a2_ref[...]) + a_bias_ref[...]) * (1.0 / GLA_TAU)
        next_gate()
        gb = seg_scan(log_a, jnp.add, 0.0)
        next_gate()
        sub_ends = group_rows(gb, (pos & (SUB - 1)) == SUB - 1, SUB)
        sub_end = spread_rows(sub_ends, SUB)
        sub_id = lax.broadcasted_iota(jnp.int32, (rows_n // SUB, 1, MIX), 0)
        prev_ends = jnp.concatenate([jnp.zeros((1, 1, MIX), F32), sub_ends[:-1]], axis=0)
        sub_start = spread_rows(jnp.where((sub_id & (n_subs - 1)) == 0, 0.0, prev_ends), SUB)
        ends_by_chunk = sub_ends.reshape(n_chunks, n_subs, MIX)
        next_gate()
        yield

        chunk_max = group_rows(u_max, last_of_chunk, CHUNK)
        chunk_b = group_rows(b, last_of_chunk, CHUNK)
        m_start = [ml_m_ref[...]]
        m_end = []
        for c in range(n_chunks):
            m_end.append(jnp.maximum(m_start[c], chunk_max[c]))
            m_start.append(chunk_b[c] + m_end[c])
        ml_m_ref[...] = m_start[n_chunks]
        m_in = chunk_rows(m_start[:n_chunks])
        m_run = jnp.maximum(m_in, u_max)
        k_exp = expand_heads(k)
        raw_scores = per_chunk(lambda c, cr: _dot_nt(q_bf[cr], k_exp[c]))
        next_gate()
        u_lane = spread_rows(group_rows(u, eye, CHUNK), CHUNK)
        decay_mat = jnp.exp(jnp.where(causal, u_lane - m_run, -jnp.inf))

        q_loc = (gq * jnp.exp(gb - sub_start)).astype(BF16)
        k_loc_exp = expand_heads(gk * jnp.exp(sub_start - gb))
        attn = jnp.where(same_sub_causal, per_chunk(lambda c, cr: _dot_nt(q_loc[cr], k_loc_exp[c])), 0.0)
        next_gate()

        scores_bf = (raw_scores * decay_mat).astype(BF16)
        v_exp = expand_heads(v)
        intra = per_chunk(lambda c, cr: _dot(scores_bf[cr], v_exp[c]))
        den_intra = _dot(scores_bf, head_ones)
        next_gate()

        k_end = gk * jnp.exp(sub_end - gb)
        for j in range(n_subs - 1):
            end_j = chunk_rows([ends_by_chunk[c, j:j + 1, :] for c in range(n_chunks)])
            q_j = jnp.where(sub_of_row > j, gq * jnp.exp(jnp.minimum(gb - end_j, 0.0)), 0.0).astype(BF16)
            k_j_exp = expand_heads(jnp.where(sub_of_row == j, k_end, 0.0))
            attn = attn + per_chunk(lambda c, cr: _dot_nt(q_j[cr], k_j_exp[c]))
            next_gate()
        yield

        k_w = k * jnp.exp(u - chunk_rows(m_end))
        k_w_bf = k_w.astype(BF16)
        k_w_sums = jnp.sum(k_w.reshape(n_chunks, CHUNK, MIX), axis=1, keepdims=True)
        v_bf = v.astype(BF16)
        state = ml_state_ref[...]
        norm_vec = ml_n_ref[...]
        inter_parts = []
        norm_in = []
        for c in range(n_chunks):
            cr = slice(c * CHUNK, (c + 1) * CHUNK)
            inter_parts.append(_dot(q_bf[cr], state.astype(BF16)))
            norm_in.append(norm_vec)
            decay = jnp.exp(m_start[c] - m_end[c])
            state = state * decay + jnp.where(block_diag, _dot_tn(k_w_bf[cr], v_bf[cr]), 0.0)
            norm_vec = norm_vec * decay + k_w_sums[c]
        ml_state_ref[...] = state
        ml_n_ref[...] = norm_vec
        den_inter = _dot((q * chunk_rows(norm_in)).astype(BF16), head_ones)
        next_gate()

        attn_bf = attn.astype(BF16)
        gv_exp = expand_heads(gv)
        g_intra = per_chunk(lambda c, cr: _dot(attn_bf[cr], gv_exp[c]))
        next_gate()

        inter = jnp.concatenate(inter_parts, axis=0)
        w_inter = jnp.exp(m_in - m_run)
        num = w_inter * inter + intra
        den = w_inter * den_inter + den_intra
        ml_h = num / jnp.maximum(jnp.abs(den), jnp.exp(-(m_run + b)))

        chunk_ends = [ends_by_chunk[c, n_subs - 1:n_subs, :] for c in range(n_chunks)]
        q_dec = (gq * jnp.exp(gb)).astype(BF16)
        k_dec = (gk * jnp.exp(chunk_rows(chunk_ends) - gb)).astype(BF16)
        gv_bf = gv.astype(BF16)
        g_state = gla_state_ref[...]
        g_inter_parts = []
        for c in range(n_chunks):
            cr = slice(c * CHUNK, (c + 1) * CHUNK)
            g_inter_parts.append(_dot_nt(q_dec[cr], g_state.astype(BF16)))
            g_state = (g_state * jnp.exp(chunk_ends[c])
                       + jnp.where(block_diag, _dot_tn(gv_bf[cr], k_dec[cr]), 0.0))
        gla_state_ref[...] = g_state
        next_gate()
        gla_h = jnp.concatenate(g_inter_parts, axis=0) + g_intra
        assert next(gate_cols, None) is None
        yield

        ml = head_rms_norm(ml_h, ml_norm_g_ref[...])
        ml_in = (ml * _sigmoid(proj_ref[rows, OFF_ML + 3 * MIX:OFF_ML + 4 * MIX])).astype(BF16)
        gl = head_rms_norm(gla_h, gla_norm_g_ref[...])
        r = proj_ref[rows, OFF_GLA + 3 * MIX:OFF_GLA + 4 * MIX]
        gl_in = (gl * (r * _sigmoid(r))).astype(BF16)

        merged = gates_ref[rows, 0:D_MODEL] * _dot(ml_in, w_branch_ref[0])
        merged = merged + gates_ref[rows, D_MODEL:2 * D_MODEL] * _dot(gl_in, w_branch_ref[1])
        merged = merged + gates_ref[rows, 2 * D_MODEL:3 * D_MODEL] * (
            _dot(branch_in_ref[2, rows, :], w_branch_ref[2]) + conf_out_b_ref[...])
        merged = merged + gates_ref[rows, 3 * D_MODEL:4 * D_MODEL] * _dot(branch_in_ref[3, rows, :], w_branch_ref[3])
        merged_bf = merged.astype(BF16)
        yield

        y = _dot(merged_bf, w_o_ref[...])
        yield

        out_ref[0, rows, :] = x + _rms_norm(y, post_g_ref[...] * gate_ref[0])

    _interleave([sub_tile_program(s) for s in range(STREAMS)], TM_LEAD)


def _resident(shape):
    zeros = (0,) * len(shape)
    return pl.BlockSpec(shape, lambda b, t: zeros, pipeline_mode=pl.Buffered(1))


def _token_mixer(x, shift, scale, gate, pre_g, post_g, w_in, if_bias, expand, w_a2, a_bias, ml_norm_g,
                 gla_norm_g, w_branch, conf_dw_w, conf_dw_b, conf_ln_g, conf_ln_b, conf_out_b,
                 sc_dw_w, merge_b, w_o):
    bsz, seq, d = x.shape
    tile = STREAMS * SUB_TILE
    assert seq % tile == 0 and SUB_TILE % CHUNK == 0 and d == D_MODEL
    x_spec = pl.BlockSpec((1, tile, d), lambda b, t: (b, t, 0))
    mod_spec = pl.BlockSpec((1, 1, d), lambda b, t: (b, 0, 0))
    params = (pre_g, post_g, w_in, if_bias, expand, w_a2, a_bias, ml_norm_g, gla_norm_g, w_branch,
              conf_dw_w, conf_dw_b, conf_ln_g, conf_ln_b, conf_out_b, sc_dw_w, merge_b, w_o)
    return pl.pallas_call(
        _token_mixer_kernel,
        grid=(bsz, seq // tile),
        in_specs=[x_spec, mod_spec, mod_spec, mod_spec] + [_resident(p.shape) for p in params],
        out_specs=x_spec,
        out_shape=jax.ShapeDtypeStruct(x.shape, x.dtype),
        scratch_shapes=[
            pltpu.VMEM((tile, 4 * MIX), F32),
            pltpu.VMEM((tile, 4 * MIX), F32),
            pltpu.VMEM((tile, 2 * MIX), F32),
            pltpu.VMEM((tile, 3 * MIX), F32),
            pltpu.VMEM((tile, 128), F32),
            pltpu.VMEM((tile, 4 * D_MODEL), F32),
            pltpu.VMEM((4, tile, MIX), BF16),
            pltpu.VMEM((MIX, MIX), F32),
            pltpu.VMEM((1, MIX), F32),
            pltpu.VMEM((1, MIX), F32),
            pltpu.VMEM((MIX, MIX), F32),
            pltpu.VMEM((CONF_HALO + tile + 8, MIX), F32),
            pltpu.VMEM((STREAMS * 7, SUB_TILE + 8, MIX), F32),
            pltpu.VMEM((CONV3_HALO + tile, MIX), F32),
        ],
        compiler_params=pltpu.CompilerParams(
            dimension_semantics=("arbitrary", "arbitrary"), vmem_limit_bytes=V7X_VMEM_LIMIT_BYTES),
        name="token_mixer",
    )(x, shift, scale, gate, *params)


def _channel_mixer_kernel(steps_per_row, x_ref, shift_ref, scale_ref, prev_gate_ref, pre_g_ref, post_g_ref,
                          w_up_ref, dw_w_ref, w_down_ref, out_ref, u_buf_ref, act_ref, prev_y_ref, prev_x_ref):
    tile = x_ref.shape[1]
    rows_n = tile // STREAMS
    n_blocks = FFN_HIDDEN // FFN_COLS
    step = pl.program_id(0)

    @pl.when(step % steps_per_row == 0)
    def _():
        u_buf_ref[0:CONV3_HALO, :] = jnp.zeros((CONV3_HALO, 2 * FFN_HIDDEN), F32)

    @pl.when(step == 0)
    def _():
        prev_y_ref[...] = jnp.zeros_like(prev_y_ref)
        prev_x_ref[...] = jnp.zeros_like(prev_x_ref)

    def sub_tile_program(s):
        r0 = s * rows_n
        rows = slice(r0, r0 + rows_n)
        u0 = CONV3_HALO + r0

        x = x_ref[0, rows, :]
        h_bf = (_rms_norm(x, pre_g_ref[...] * (1.0 + scale_ref[0])) + shift_ref[0]).astype(BF16)
        yield

        def conv3(cols):
            return (dw_w_ref[0:1, cols] * u_buf_ref[u0 - 2:u0 - 2 + rows_n, cols]
                    + dw_w_ref[1:2, cols] * u_buf_ref[u0 - 1:u0 - 1 + rows_n, cols]
                    + dw_w_ref[2:3, cols] * u_buf_ref[u0:u0 + rows_n, cols])

        for n in range(n_blocks):
            cols_a = slice(n * FFN_COLS, (n + 1) * FFN_COLS)
            cols_v = slice(FFN_HIDDEN + n * FFN_COLS, FFN_HIDDEN + (n + 1) * FFN_COLS)
            u_buf_ref[u0:u0 + rows_n, cols_a] = _dot(h_bf, w_up_ref[:, cols_a])
            u_buf_ref[u0:u0 + rows_n, cols_v] = _dot(h_bf, w_up_ref[:, cols_v])
            yield
            a = conv3(cols_a)
            v = conv3(cols_v)
            act_ref[rows, cols_a] = (a * _sigmoid(a) * v).astype(BF16)
            if s == STREAMS - 1:
                for cols in (cols_a, cols_v):
                    u_buf_ref[0:CONV3_HALO, cols] = u_buf_ref[tile:tile + CONV3_HALO, cols]
            yield

        y = None
        for n in range(0, n_blocks, DOWN_BLOCKS):
            k_rows = slice(n * FFN_COLS, min(n + DOWN_BLOCKS, n_blocks) * FFN_COLS)
            part = _dot(act_ref[rows, k_rows], w_down_ref[k_rows, :])
            y = part if y is None else y + part
            yield

        out_ref[0, rows, :] = prev_x_ref[rows, :] + _rms_norm(prev_y_ref[rows, :], post_g_ref[...] * prev_gate_ref[0])
        prev_y_ref[rows, :] = y
        prev_x_ref[rows, :] = x

    _interleave([sub_tile_program(s) for s in range(STREAMS)], FFN_LEAD)


def _channel_mixer(x, shift, scale, gate, pre_g, post_g, w_up, dw_w, w_down):
    bsz, seq, d = x.shape
    tile = STREAMS * SUB_TILE
    assert seq % tile == 0 and d == D_MODEL
    steps_per_row = seq // tile
    n_tiles = bsz * steps_per_row

    def this_tile(i):
        return jnp.minimum(i, n_tiles - 1)

    def prev_tile(i):
        return jnp.maximum(i - 1, 0)

    params = (pre_g, post_g, w_up, dw_w, w_down)
    out = pl.pallas_call(
        functools.partial(_channel_mixer_kernel, steps_per_row),
        grid=(n_tiles + 1,),
        in_specs=[
            pl.BlockSpec((1, tile, d), lambda i: (this_tile(i), 0, 0)),
            pl.BlockSpec((1, 1, d), lambda i: (this_tile(i) // steps_per_row, 0, 0)),
            pl.BlockSpec((1, 1, d), lambda i: (this_tile(i) // steps_per_row, 0, 0)),
            pl.BlockSpec((1, 1, d), lambda i: (prev_tile(i) // steps_per_row, 0, 0)),
        ] + [pl.BlockSpec(p.shape, lambda i, n=p.ndim: (0,) * n, pipeline_mode=pl.Buffered(1)) for p in params],
        out_specs=pl.BlockSpec((1, tile, d), lambda i: (prev_tile(i), 0, 0)),
        out_shape=jax.ShapeDtypeStruct((n_tiles, tile, d), x.dtype),
        scratch_shapes=[
            pltpu.VMEM((CONV3_HALO + tile, 2 * FFN_HIDDEN), F32),
            pltpu.VMEM((tile, FFN_HIDDEN), BF16),
            pltpu.VMEM((tile, d), F32),
            pltpu.VMEM((tile, d), F32),
        ],
        compiler_params=pltpu.CompilerParams(
            dimension_semantics=("arbitrary",), vmem_limit_bytes=V7X_VMEM_LIMIT_BYTES),
        name="channel_mixer",
    )(x.reshape(n_tiles, tile, d), shift, scale, gate, *params)
    return out.reshape(bsz, seq, d)


def _pad_rows(a, n):
    return jnp.pad(a, ((0, n - a.shape[0]), (0, 0)))


def _pack_w_in(w):
    ml = w[:, 0:1024]
    ml_i = w[:, 1024:1028]
    ml_f = w[:, 1028:1032]
    gla = w[:, 1032:2056]
    gla_a = w[:, 2056:2072]
    conf = w[:, 2072:2584]
    sc = w[:, 2584:3352]
    gate = w[:, 3352:7448]
    small = jnp.concatenate([gla_a, ml_i, ml_f], axis=1)
    assert gla_a.shape[1] == SMALL_I and SMALL_F == SMALL_I + HEADS
    packed = jnp.concatenate([
        ml, gla, conf, sc, jnp.pad(small, ((0, 0), (0, 128 - small.shape[1]))), 0.5 * gate], axis=1)
    assert packed.shape[1] == PACKED_COLS
    return packed.astype(BF16)


def _gate_spread_matrix():
    lane = jnp.arange(128)[:, None]
    col = jnp.arange(2 * MIX)[None, :]
    is_forget = col >= MIX
    head = (col % MIX) // HEAD_DIM
    one = lane == jnp.where(is_forget, SMALL_F, SMALL_I) + head
    half = jnp.where(one, 1.0, 0.0).astype(BF16)
    return jnp.concatenate([half, half], axis=0)


def kernel(x, c, ada_w, ada_b, tm_pre_g, tm_post_g, cm_pre_g, cm_post_g, w_in, ml_i_bias, ml_f_bias, ml_norm_g, w_ml_out, gla_w_a2, gla_a_bias, gla_norm_g, w_gla_out, conf_dw_w, conf_dw_b, conf_ln_g, conf_ln_b, w_conf_out, conf_out_b, sc_dw_w, w_sc_out, merge_gate_b, w_o, ffn_w_up, ffn_dw_w, ffn_w_down):
    bsz = x.shape[0]
    depth = ada_w.shape[0]
    d = D_MODEL
    mod = _modulation(c, ada_w, ada_b)

    def row(a):
        return a.reshape(1, -1)

    for l in range(depth):
        m = mod[l].reshape(bsz, 6, 1, d)
        sh_t, sc_t, g_t, sh_c, sc_c, g_c = (m[:, i] for i in range(6))
        if_bias = jnp.pad(jnp.concatenate([ml_i_bias[l], ml_f_bias[l]]), (SMALL_I, 128 - SMALL_I - 2 * HEADS))[None]
        w_branch = jnp.stack([w_ml_out[l], w_gla_out[l], w_conf_out[l], w_sc_out[l]]).astype(BF16)
        x = _token_mixer(
            x, sh_t, sc_t, g_t, row(tm_pre_g[l]), row(tm_post_g[l]), _pack_w_in(w_in[l]),
            _pad_rows(if_bias, 8), _gate_spread_matrix(), _pad_rows(gla_w_a2[l], 128).astype(BF16), row(gla_a_bias[l]),
            row(ml_norm_g[l]), row(gla_norm_g[l]), w_branch, _pad_rows(conf_dw_w[l], CONF_HALO),
            row(conf_dw_b[l]), row(conf_ln_g[l]), row(conf_ln_b[l]), row(conf_out_b[l]),
            _pad_rows(sc_dw_w[l], 8), row(0.5 * merge_gate_b[l]), (0.5 * w_o[l]).astype(BF16))
        x = _channel_mixer(
            x, sh_c, sc_c, g_c, row(cm_pre_g[l]), row(cm_post_g[l]), ffn_w_up[l].astype(BF16),
            _pad_rows(ffn_dw_w[l], 8), ffn_w_down[l].astype(BF16))
    return x
```
